```python
import math
import jax
import jax.numpy as jnp
from jax import lax
import numpy as np

D_MODEL = 2048
BATCH = 32
SEQ = 256
DEPTH = 4
DEC_BATCH = 2
DEC_SEQ = 4096
PAST_LEN = 512

GRID_W = 64
N_EVEN = (DEPTH + 1) // 2
N_ODD = DEPTH // 2
N_MOD = 9
D_FF = 5632
EPS = 1e-6
MIX_HALF = D_MODEL // 2

S5_WIDTH = MIX_HALF
S5_H = 16
S5_GROUPS = S5_WIDTH // S5_H
S5_P = 64
S5_LAM_RE_MAX = -1e-4
HY_WIDTH = MIX_HALF
HY_ORDER = 2
HY_SHORT = 3
HY_BANDS = 16
HY_EMB = 1 + 2 * HY_BANDS
HY_FFN = 64
HY_DECAY_MIN = -math.log(1e-2) / 1.5
HY_DECAY_MAX = -math.log(1e-2) / 0.3
RG_WIDTH = MIX_HALF
RG_BLOCKS = 16
RG_BW = RG_WIDTH // RG_BLOCKS
RG_CONV = 4
RG_C = 8.0
NA_HD = 64
NA_HEADS = MIX_HALF // NA_HD
NA_WIDTH = NA_HEADS * NA_HD
NA_WIN_R = 8
NA_WIN_C = 16
Q_BLOCK = 128

EVEN_IN = S5_WIDTH + 3 * HY_WIDTH
EVEN_OUT = S5_WIDTH + HY_WIDTH
ODD_IN = 2 * RG_WIDTH + 3 * NA_WIDTH
ODD_OUT = RG_WIDTH + NA_WIDTH

kernel_name = 'hybrid_diffusion_prefix_trunk_step'


def rmsnorm(x, g):
    xf = x.astype(jnp.float32)
    y = xf * lax.rsqrt(jnp.mean(xf * xf, axis=-1, keepdims=True) + EPS)
    return (y * g.astype(jnp.float32)).astype(x.dtype)


def modulate(x, g, shift, scale):
    return rmsnorm(x, g) * (1 + scale) + shift


def swiglu(h, wg, wu, wd):
    return (jax.nn.silu(h @ wg) * (h @ wu)) @ wd


def adaln(cond, w, b):
    m = jax.nn.silu(cond) @ w + b
    return jnp.split(m[:, None, :], N_MOD, axis=-1)


def pre_mix(x, m, g, wg, wu, wd):
    x = x + 0.5 * m[2] * swiglu(modulate(x, g[0], m[0], m[1]), wg[0], wu[0], wd[0])
    return x, modulate(x, g[1], m[3], m[4])


def post_mix(x, y, m, g, wg, wu, wd):
    x = x + m[5] * y
    return x + 0.5 * m[8] * swiglu(modulate(x, g[2], m[6], m[7]), wg[1], wu[1], wd[1])


def depthwise_conv(x, w, b, left):
    K = w.shape[0]
    L = x.shape[1]
    xp = jnp.pad(x, ((0, 0), (left, K - 1 - left), (0, 0)))
    y = b + xp[:, 0:L] * w[0]
    for j in range(1, K):
        y = y + xp[:, j:j + L] * w[j]
    return y


def _cmul(ar, ai, br, bi):
    return ar * br - ai * bi, ar * bi + ai * br


def _complex_combine(e1, e2):
    a1r, a1i, b1r, b1i = e1
    a2r, a2i, b2r, b2i = e2
    ar, ai = _cmul(a2r, a2i, a1r, a1i)
    br, bi = _cmul(a2r, a2i, b1r, b1i)
    return ar, ai, br + b2r, bi + b2i


def _real_combine(e1, e2):
    a1, b1 = e1
    a2, b2 = e2
    return a1 * a2, a2 * b1 + b2


def complex_scan(ar, ai, br, bi, h0r, h0i):
    ir, ii = _cmul(ar, ai, h0r, h0i)
    br = br.at[:, 0].add(ir)
    bi = bi.at[:, 0].add(ii)
    elems = (jnp.broadcast_to(ar, br.shape), jnp.broadcast_to(ai, bi.shape), br, bi)
    _, _, hr, hi = lax.associative_scan(_complex_combine, elems, axis=1)
    return hr, hi


def real_scan(a, b, h0):
    b = b.at[:, 0].add(a[:, 0] * h0)
    _, h = lax.associative_scan(_real_combine, (a, b), axis=1)
    return h


def s5_mixer(u, lam_re, lam_im, log_dt, b_re, b_im, c_re, c_im, d_skip, glu_w, glu_b, h0, want_state):
    f32 = jnp.float32
    bsz, L, _ = u.shape
    uf = u.astype(f32).reshape(bsz, L, S5_GROUPS, S5_H)
    h0 = h0.astype(f32)
    y = d_skip.astype(f32) * uf
    finals = []
    for d in range(2):
        lr = jnp.minimum(lam_re[d].astype(f32), S5_LAM_RE_MAX)
        li = lam_im[d].astype(f32)
        dt = jnp.exp(log_dt[d].astype(f32))[:, None]
        mag = jnp.exp(lr * dt)
        abr, abi = mag * jnp.cos(li * dt), mag * jnp.sin(li * dt)
        den = lr * lr + li * li
        cr = ((abr - 1.0) * lr + abi * li) / den
        ci = (abi * lr - (abr - 1.0) * li) / den
        br_, bi_ = b_re[d].astype(f32), b_im[d].astype(f32)
        bbr = cr[..., None] * br_ - ci[..., None] * bi_
        bbi = cr[..., None] * bi_ + ci[..., None] * br_
        ud = uf if d == 0 else uf[:, ::-1]
        bur = jnp.einsum('gph,blgh->blgp', bbr, ud)
        bui = jnp.einsum('gph,blgh->blgp', bbi, ud)
        hr, hi = complex_scan(abr, abi, bur, bui, h0[:, d, ..., 0], h0[:, d, ..., 1])
        if want_state:
            finals.append(jnp.stack([hr[:, -1], hi[:, -1]], axis=-1))
        if d == 1:
            hr, hi = hr[:, ::-1], hi[:, ::-1]
        y = y + jnp.einsum('ghp,blgp->blgh', c_re[d].astype(f32), hr) - jnp.einsum('ghp,blgp->blgh', c_im[d].astype(f32), hi)
    z = jax.nn.gelu(y.reshape(bsz, L, S5_WIDTH))
    out = z * jax.nn.sigmoid(z @ glu_w.astype(f32) + glu_b.astype(f32))
    fin = jnp.stack(finals, axis=1) if want_state else None
    return out.astype(u.dtype), fin


def hyena_filters(L, w1, b1, w2, b2, w3, freq, decay):
    f32 = jnp.float32
    t = jnp.arange(L, dtype=f32)
    t01 = t / L
    bands = jnp.linspace(1e-4, HY_BANDS - 1, HY_BANDS, dtype=f32)
    ang = (2.0 * math.pi / L) * t[:, None] * bands[None, :]
    z = jnp.concatenate([t01[:, None], jnp.cos(ang), -jnp.sin(ang)], axis=-1)
    fr = freq.astype(f32)
    h = jnp.sin(fr * (z @ w1.astype(f32) + b1.astype(f32)))
    h = jnp.sin(fr * (h @ w2.astype(f32) + b2.astype(f32)))
    h = (h @ w3.astype(f32)).reshape(L, HY_ORDER, 2, HY_WIDTH)
    h = h * jnp.exp(-t01[:, None, None, None] * jnp.abs(decay.astype(f32)))
    k = jnp.concatenate([h[:, :, 0], jnp.zeros((1, HY_ORDER, HY_WIDTH), f32), h[:0:-1, :, 1]], axis=0)
    return k / jnp.sum(jnp.abs(k), axis=0, keepdims=True)


def hyena_mixer(p, conv_w, conv_b, w1, b1, w2, b2, w3, freq, decay, bias):
    f32 = jnp.float32
    bsz, L, _ = p.shape
    u = depthwise_conv(p, conv_w, conv_b, HY_SHORT // 2).astype(f32)
    v, g1, g2 = u[..., :HY_WIDTH], u[..., HY_WIDTH:2 * HY_WIDTH], u[..., 2 * HY_WIDTH:]
    kf = jnp.fft.rfft(hyena_filters(L, w1, b1, w2, b2, w3, freq, decay), axis=0)
    z = v
    for n, gate in enumerate((g1, g2)):
        zf = jnp.fft.rfft(z, n=2 * L, axis=1)
        conv = jnp.fft.irfft(zf * kf[:, n], n=2 * L, axis=1)[:, :L]
        z = gate * (conv + bias[n].astype(f32) * z)
    return z.astype(p.dtype)


def rglru_mixer(xb, gb, conv_w, conv_b, w_a, b_a, w_i, b_i, lam, h0, want_state):
    f32 = jnp.float32
    bsz, L, _ = xb.shape
    xc = depthwise_conv(xb, conv_w, conv_b, RG_CONV // 2).astype(f32)
    h0 = h0.astype(f32)
    hsum = jnp.zeros_like(xc)
    finals = []
    for d in range(2):
        xd = xc if d == 0 else xc[:, ::-1]
        xs = xd.reshape(bsz, L, RG_BLOCKS, RG_BW)
        r = jax.nn.sigmoid(jnp.einsum('blnc,ncd->blnd', xs, w_a[d].astype(f32)) + b_a[d].astype(f32)).reshape(bsz, L, RG_WIDTH)
        i = jax.nn.sigmoid(jnp.einsum('blnc,ncd->blnd', xs, w_i[d].astype(f32)) + b_i[d].astype(f32)).reshape(bsz, L, RG_WIDTH)
        log_a = -RG_C * r * jax.nn.softplus(-lam[d].astype(f32))
        a = jnp.exp(log_a)
        b = jnp.sqrt(-jnp.expm1(2.0 * log_a)) * (i * xd)
        h = real_scan(a, b, h0[:, d])
        if want_state:
            finals.append(h[:, -1])
        if d == 1:
            h = h[:, ::-1]
        hsum = hsum + h
    y = hsum * jax.nn.gelu(gb.astype(f32))
    fin = jnp.stack(finals, axis=1) if want_state else None
    return y.astype(xb.dtype), fin


def ctx_attention(q, k, v):
    bsz, Lc, H, hd = q.shape
    scale = hd ** -0.5
    qb = jnp.moveaxis(q.reshape(bsz, Lc // Q_BLOCK, Q_BLOCK, H, hd), 1, 0)

    def block(qi):
        s = jnp.einsum('bqhd,bkhd->bhqk', qi, k).astype(jnp.float32) * scale
        p = jax.nn.softmax(s, axis=-1).astype(v.dtype)
        return jnp.einsum('bhqk,bkhd->bqhd', p, v)

    o = lax.map(block, qb)
    return jnp.moveaxis(o, 0, 1).reshape(bsz, Lc, H * hd)


def na_attention(q, k, v, k_ctx, v_ctx, rpb):
    bsz, L, H, hd = q.shape
    rows = L // GRID_W
    wr = min(NA_WIN_R, rows)
    wc = NA_WIN_C
    scale = hd ** -0.5
    qg = q.reshape(bsz, rows, GRID_W, H, hd)
    kg = k.reshape(bsz, rows, GRID_W, H, hd)
    vg = v.reshape(bsz, rows, GRID_W, H, hd)
    col = jnp.arange(GRID_W)
    col_idx = jnp.clip(col - wc // 2, 0, GRID_W - wc)[:, None] + jnp.arange(wc)[None, :]
    col_bias = rpb[:, :, col_idx - col[:, None] + NA_WIN_C - 1]

    def row_block(r):
        rs = jnp.clip(r - wr // 2, 0, rows - wr)
        kr = lax.dynamic_slice_in_dim(kg, rs, wr, axis=1)[:, :, col_idx]
        vr = lax.dynamic_slice_in_dim(vg, rs, wr, axis=1)[:, :, col_idx]
        qr = lax.dynamic_index_in_dim(qg, r, axis=1, keepdims=False)
        bias = col_bias[:, rs + jnp.arange(wr) - r + NA_WIN_R - 1]
        s_loc = jnp.einsum('bqhd,brqchd->bhqrc', qr, kr).astype(jnp.float32) * scale
        s_loc = s_loc + jnp.transpose(bias, (0, 2, 1, 3)).astype(jnp.float32)[None]
        s_ctx = jnp.einsum('bqhd,bkhd->bhqk', qr, k_ctx).astype(jnp.float32) * scale
        s = jnp.concatenate([s_loc.reshape(bsz, H, GRID_W, wr * wc), s_ctx], axis=-1)
        p = jax.nn.softmax(s, axis=-1).astype(v.dtype)
        p_loc = p[..., :wr * wc].reshape(bsz, H, GRID_W, wr, wc)
        return (jnp.einsum('bhqrc,brqchd->bqhd', p_loc, vr)
                + jnp.einsum('bhqk,bkhd->bqhd', p[..., wr * wc:], v_ctx))

    o = lax.map(row_block, jnp.arange(rows))
    return jnp.moveaxis(o, 0, 1).reshape(bsz, L, H * hd)


def setup_inputs(seed: int = 0) -> dict:
    key = jax.random.key(seed)
    ks = iter(jax.random.split(key, 80))
    f32 = jnp.float32

    def nrm(shape, s):
        return s * jax.random.normal(next(ks), shape, f32)

    x_prompt = nrm((BATCH, SEQ, D_MODEL), 1.0)
    x_sample = nrm((DEC_BATCH, DEC_SEQ, D_MODEL), 1.0)
    cache_na_k = nrm((DEC_BATCH, N_ODD, PAST_LEN, NA_HEADS, NA_HD), 1.0)
    cache_na_v = nrm((DEC_BATCH, N_ODD, PAST_LEN, NA_HEADS, NA_HD), 1.0)
    state_s5 = nrm((DEC_BATCH, N_EVEN, 2, S5_GROUPS, S5_P, 2), 0.1)
    state_rglru = nrm((DEC_BATCH, N_ODD, 2, RG_WIDTH), 0.5)
    c = nrm((DEC_BATCH, D_MODEL), 1.0)
    c_ctx = nrm((D_MODEL,), 1.0)
    norm_g = 1.0 + nrm((DEPTH, 3, D_MODEL), 0.1)
    mod_w = nrm((DEPTH, D_MODEL, N_MOD * D_MODEL), 0.5 * D_MODEL ** -0.5)
    mod_b = nrm((DEPTH, N_MOD * D_MODEL), 0.01)
    ffn_wg = nrm((DEPTH, 2, D_MODEL, D_FF), D_MODEL ** -0.5)
    ffn_wu = nrm((DEPTH, 2, D_MODEL, D_FF), D_MODEL ** -0.5)
    ffn_wd = nrm((DEPTH, 2, D_FF, D_MODEL), D_FF ** -0.5)
    even_w_in = nrm((N_EVEN, D_MODEL, EVEN_IN), D_MODEL ** -0.5)
    even_w_out = nrm((N_EVEN, EVEN_OUT, D_MODEL), EVEN_OUT ** -0.5)
    s5_lam_re = -0.5 + nrm((N_EVEN, 2, S5_GROUPS, S5_P), 0.01)
    s5_lam_im = math.pi * jnp.arange(S5_P, dtype=f32) + nrm((N_EVEN, 2, S5_GROUPS, S5_P), 0.01)
    s5_log_dt = jax.random.uniform(next(ks), (N_EVEN, 2, S5_GROUPS), f32, math.log(1e-3), math.log(1e-1))
    s5_b_re = nrm((N_EVEN, 2, S5_GROUPS, S5_P, S5_H), (2 * S5_H) ** -0.5)
    s5_b_im = nrm((N_EVEN, 2, S5_GROUPS, S5_P, S5_H), (2 * S5_H) ** -0.5)
    s5_c_re = nrm((N_EVEN, 2, S5_GROUPS, S5_H, S5_P), S5_P ** -0.5)
    s5_c_im = nrm((N_EVEN, 2, S5_GROUPS, S5_H, S5_P), S5_P ** -0.5)
    s5_d = nrm((N_EVEN, S5_GROUPS, S5_H), 1.0)
    s5_glu_w = nrm((N_EVEN, S5_WIDTH, S5_WIDTH), S5_WIDTH ** -0.5)
    s5_glu_b = nrm((N_EVEN, S5_WIDTH), 0.01)
    hy_conv_w = nrm((N_EVEN, HY_SHORT, 3 * HY_WIDTH), HY_SHORT ** -0.5)
    hy_conv_b = nrm((N_EVEN, 3 * HY_WIDTH), 0.01)
    hy_w1 = nrm((N_EVEN, HY_EMB, HY_FFN), HY_EMB ** -0.5)
    hy_b1 = nrm((N_EVEN, HY_FFN), 0.1)
    hy_w2 = nrm((N_EVEN, HY_FFN, HY_FFN), HY_FFN ** -0.5)
    hy_b2 = nrm((N_EVEN, HY_FFN), 0.1)
    hy_w3 = nrm((N_EVEN, HY_FFN, HY_ORDER * 2 * HY_WIDTH), HY_FFN ** -0.5)
    hy_freq = 1.0 + nrm((N_EVEN, HY_FFN), 0.1)
    hy_decay = jnp.linspace(HY_DECAY_MIN, HY_DECAY_MAX, HY_WIDTH, dtype=f32) + nrm((N_EVEN, HY_ORDER, 2, HY_WIDTH), 0.1)
    hy_bias = nrm((N_EVEN, HY_ORDER, HY_WIDTH), 1.0)
    odd_w_in = nrm((N_ODD, D_MODEL, ODD_IN), D_MODEL ** -0.5)
    odd_w_out = nrm((N_ODD, ODD_OUT, D_MODEL), ODD_OUT ** -0.5)
    rg_conv_w = nrm((N_ODD, RG_CONV, RG_WIDTH), RG_CONV ** -0.5)
    rg_conv_b = nrm((N_ODD, RG_WIDTH), 0.01)
    rg_wa = nrm((N_ODD, 2, RG_BLOCKS, RG_BW, RG_BW), RG_BW ** -0.5)
    rg_ba = nrm((N_ODD, 2, RG_BLOCKS, RG_BW), 0.1)
    rg_wi = nrm((N_ODD, 2, RG_BLOCKS, RG_BW, RG_BW), RG_BW ** -0.5)
    rg_bi = nrm((N_ODD, 2, RG_BLOCKS, RG_BW), 0.1)
    a_c = jax.random.uniform(next(ks), (N_ODD, 2, RG_WIDTH), f32, 0.9, 0.999)
    s_a = a_c ** (1.0 / RG_C)
    rg_lam = jnp.log(s_a) - jnp.log1p(-s_a)
    na_qn = 1.0 + nrm((N_ODD, NA_HD), 0.1)
    na_kn = 1.0 + nrm((N_ODD, NA_HD), 0.1)
    na_rpb = nrm((N_ODD, NA_HEADS, 2 * NA_WIN_R - 1, 2 * NA_WIN_C - 1), 0.1)
    return {'x_prompt': x_prompt, 'x_sample': x_sample, 'cache_na_k': cache_na_k, 'cache_na_v': cache_na_v,
            'state_s5': state_s5, 'state_rglru': state_rglru, 'c': c, 'c_ctx': c_ctx,
            'norm_g': norm_g, 'mod_w': mod_w, 'mod_b': mod_b, 'ffn_wg': ffn_wg, 'ffn_wu': ffn_wu, 'ffn_wd': ffn_wd,
            'even_w_in': even_w_in, 'even_w_out': even_w_out,
            's5_lam_re': s5_lam_re, 's5_lam_im': s5_lam_im, 's5_log_dt': s5_log_dt, 's5_b_re': s5_b_re, 's5_b_im': s5_b_im,
            's5_c_re': s5_c_re, 's5_c_im': s5_c_im, 's5_d': s5_d, 's5_glu_w': s5_glu_w, 's5_glu_b': s5_glu_b,
            'hy_conv_w': hy_conv_w, 'hy_conv_b': hy_conv_b, 'hy_w1': hy_w1, 'hy_b1': hy_b1, 'hy_w2': hy_w2, 'hy_b2': hy_b2,
            'hy_w3': hy_w3, 'hy_freq': hy_freq, 'hy_decay': hy_decay, 'hy_bias': hy_bias,
            'odd_w_in': odd_w_in, 'odd_w_out': odd_w_out, 'rg_conv_w': rg_conv_w, 'rg_conv_b': rg_conv_b,
            'rg_wa': rg_wa, 'rg_ba': rg_ba, 'rg_wi': rg_wi, 'rg_bi': rg_bi, 'rg_lam': rg_lam,
            'na_qn': na_qn, 'na_kn': na_kn, 'na_rpb': na_rpb}


def reference(x_prompt, x_sample, cache_na_k, cache_na_v, state_s5, state_rglru, c, c_ctx,
              norm_g, mod_w, mod_b, ffn_wg, ffn_wu, ffn_wd, even_w_in, even_w_out,
              s5_lam_re, s5_lam_im, s5_log_dt, s5_b_re, s5_b_im, s5_c_re, s5_c_im, s5_d, s5_glu_w, s5_glu_b,
              hy_conv_w, hy_conv_b, hy_w1, hy_b1, hy_w2, hy_b2, hy_w3, hy_freq, hy_decay, hy_bias,
              odd_w_in, odd_w_out, rg_conv_w, rg_conv_b, rg_wa, rg_ba, rg_wi, rg_bi, rg_lam,
              na_qn, na_kn, na_rpb):

    def even_mix(h, ei, s5_h0, want_state):
        proj = h @ even_w_in[ei]
        y_a, s5_fin = s5_mixer(proj[..., :S5_WIDTH], s5_lam_re[ei], s5_lam_im[ei], s5_log_dt[ei],
                               s5_b_re[ei], s5_b_im[ei], s5_c_re[ei], s5_c_im[ei], s5_d[ei],
                               s5_glu_w[ei], s5_glu_b[ei], s5_h0, want_state)
        y_b = hyena_mixer(proj[..., S5_WIDTH:], hy_conv_w[ei], hy_conv_b[ei], hy_w1[ei], hy_b1[ei],
                          hy_w2[ei], hy_b2[ei], hy_w3[ei], hy_freq[ei], hy_decay[ei], hy_bias[ei])
        return jnp.concatenate([y_a, y_b], axis=-1) @ even_w_out[ei], s5_fin

    def odd_mix(h, oi, rg_h0, kv_ctx):
        want_state = kv_ctx is None
        proj = h @ odd_w_in[oi]
        bsz, L, _ = proj.shape
        y_c, rg_fin = rglru_mixer(proj[..., :RG_WIDTH], proj[..., RG_WIDTH:2 * RG_WIDTH], rg_conv_w[oi],
                                  rg_conv_b[oi], rg_wa[oi], rg_ba[oi], rg_wi[oi], rg_bi[oi], rg_lam[oi],
                                  rg_h0, want_state)
        qkv = proj[..., 2 * RG_WIDTH:].reshape(bsz, L, 3, NA_HEADS, NA_HD)
        q = rmsnorm(qkv[:, :, 0], na_qn[oi])
        k = rmsnorm(qkv[:, :, 1], na_kn[oi])
        v = qkv[:, :, 2]
        if want_state:
            y_d = ctx_attention(q, k, v)
            aux = (rg_fin, k, v)
        else:
            y_d = na_attention(q, k, v, kv_ctx[0], kv_ctx[1], na_rpb[oi])
            aux = None
        return jnp.concatenate([y_c, y_d], axis=-1) @ odd_w_out[oi], aux

    x = x_prompt
    bp = x.shape[0]
    s5_new, rg_new, k_new, v_new = [], [], [], []
    for li in range(DEPTH):
        m = adaln(c_ctx[None], mod_w[li], mod_b[li])
        x, h = pre_mix(x, m, norm_g[li], ffn_wg[li], ffn_wu[li], ffn_wd[li])
        if li % 2 == 0:
            y, s5_fin = even_mix(h, li // 2, jnp.zeros((bp, 2, S5_GROUPS, S5_P, 2), jnp.float32), True)
            s5_new.append(s5_fin.astype(x_prompt.dtype))
        else:
            y, (rg_fin, k_c, v_c) = odd_mix(h, li // 2, jnp.zeros((bp, 2, RG_WIDTH), jnp.float32), None)
            rg_new.append(rg_fin.astype(x_prompt.dtype))
            k_new.append(k_c)
            v_new.append(v_c)
        x = post_mix(x, y, m, norm_g[li], ffn_wg[li], ffn_wu[li], ffn_wd[li])
    y_prompt = x

    x = x_sample
    for li in range(DEPTH):
        m = adaln(c, mod_w[li], mod_b[li])
        x, h = pre_mix(x, m, norm_g[li], ffn_wg[li], ffn_wu[li], ffn_wd[li])
        if li % 2 == 0:
            y, _ = even_mix(h, li // 2, state_s5[:, li // 2], False)
        else:
            oi = li // 2
            y, _ = odd_mix(h, oi, state_rglru[:, oi], (cache_na_k[:, oi], cache_na_v[:, oi]))
        x = post_mix(x, y, m, norm_g[li], ffn_wg[li], ffn_wu[li], ffn_wd[li])
    y_sample = x

    new_na_k = jnp.stack(k_new, axis=1)
    new_na_v = jnp.stack(v_new, axis=1)
    new_s5 = jnp.stack(s5_new, axis=1)
    new_rglru = jnp.stack(rg_new, axis=1)
    return (y_prompt, y_sample, new_na_k, new_na_v, new_s5, new_rglru)
```

```python
import functools
import math

import jax
import jax.numpy as jnp
from jax import lax
from jax.experimental import pallas as pl
from jax.experimental.pallas import tpu as pltpu

F32 = jnp.float32
BF16 = jnp.bfloat16

D_MODEL = 2048
DEPTH = 4
N_MOD = 9
D_FF = 5632
EPS = 1e-6
MIX_HALF = D_MODEL // 2
GRID_W = 64
S5_WIDTH = MIX_HALF
S5_H = 16
S5_GROUPS = S5_WIDTH // S5_H
S5_P = 64
S5_LAM_RE_MAX = -1e-4
HY_WIDTH = MIX_HALF
HY_ORDER = 2
HY_SHORT = 3
HY_BANDS = 16
RG_WIDTH = MIX_HALF
RG_BLOCKS = 16
RG_BW = RG_WIDTH // RG_BLOCKS
RG_CONV = 4
RG_C = 8.0
NA_HD = 64
NA_HEADS = MIX_HALF // NA_HD
NA_WIDTH = NA_HEADS * NA_HD
NA_WIN_R = 8
NA_WIN_C = 16
Q_BLOCK = 128

VMEM_LIMIT_BYTES = 56 * 1024 * 1024
COND_ROWS = 8
TOKEN_TILE = 512
FF_TILE = 512


def _cparams(*sem):
    return pltpu.CompilerParams(dimension_semantics=sem, vmem_limit_bytes=VMEM_LIMIT_BYTES)


def _adaln_kernel(cond_ref, w_ref, b_ref, o_ref):
    c = cond_ref[...]
    s = (c * jax.nn.sigmoid(c)).astype(BF16)
    o_ref[...] = jnp.dot(s, w_ref[...].astype(BF16), preferred_element_type=F32) + b_ref[...]


def adaln_all(cond, mod_w, mod_b):
    tn = 1024
    n_out = N_MOD * D_MODEL
    out = pl.pallas_call(
        _adaln_kernel,
        grid=(DEPTH, n_out // tn),
        in_specs=[
            pl.BlockSpec((COND_ROWS, D_MODEL), lambda l, j: (0, 0)),
            pl.BlockSpec((None, D_MODEL, tn), lambda l, j: (l, 0, j)),
            pl.BlockSpec((None, 1, tn), lambda l, j: (l, 0, j)),
        ],
        out_specs=pl.BlockSpec((None, COND_ROWS, tn), lambda l, j: (l, 0, j)),
        out_shape=jax.ShapeDtypeStruct((DEPTH, COND_ROWS, n_out), F32),
        compiler_params=_cparams("parallel", "parallel"),
        name="adaln",
    )(cond, mod_w, mod_b.reshape(DEPTH, 1, n_out))
    return out.reshape(DEPTH, COND_ROWS, N_MOD, D_MODEL)


def _rms_modulate(x, g, shift, scale):
    ms = jnp.mean(x * x, axis=-1, keepdims=True)
    return x * lax.rsqrt(ms + EPS) * g * (1.0 + scale) + shift


def _ffn_kernel(x_ref, m_ref, g_ref, wg_ref, wu_ref, wd_ref, o_ref, hn_ref, h_s, acc_s,
                *, n_ff, in_rows, gate_row, g_row, next_rows, next_g_row):
    k = pl.program_id(1)

    @pl.when(k == 0)
    def _():
        h = _rms_modulate(x_ref[...], g_ref[g_row:g_row + 1, :],
                          m_ref[in_rows[0]:in_rows[0] + 1, :], m_ref[in_rows[1]:in_rows[1] + 1, :])
        h_s[...] = h.astype(BF16)
        acc_s[...] = jnp.zeros_like(acc_s)

    h = h_s[...]
    gt = jnp.dot(h, wg_ref[...], preferred_element_type=F32)
    ut = jnp.dot(h, wu_ref[...], preferred_element_type=F32)
    a = (gt * jax.nn.sigmoid(gt) * ut).astype(BF16)
    acc_s[...] += jnp.dot(a, wd_ref[...], preferred_element_type=F32)

    @pl.when(k == n_ff - 1)
    def _():
        xn = x_ref[...] + 0.5 * m_ref[gate_row:gate_row + 1, :] * acc_s[...]
        o_ref[...] = xn
        hn = _rms_modulate(xn, g_ref[next_g_row:next_g_row + 1, :],
                           m_ref[next_rows[0]:next_rows[0] + 1, :], m_ref[next_rows[1]:next_rows[1] + 1, :])
        hn_ref[...] = hn.astype(BF16)


def _cond_row(i, tm, n_prompt_tok, sample_len):
    return jnp.where(i * tm < n_prompt_tok, 0, 1 + (i * tm - n_prompt_tok) // sample_len)


def ffn_block(x, m_all, norm_g, wg, wu, wd, li, half, n_prompt_tok, sample_len):
    n_tok = x.shape[0]
    tm, tf = TOKEN_TILE, FF_TILE
    n_ff = D_FF // tf
    if half == 0:
        cfg = dict(in_rows=(0, 1), gate_row=2, g_row=0, next_rows=(3, 4), next_g_row=1)
    else:
        cfg = dict(in_rows=(6, 7), gate_row=8, g_row=2, next_rows=(3, 4), next_g_row=1)
    row = functools.partial(_cond_row, tm=tm, n_prompt_tok=n_prompt_tok, sample_len=sample_len)
    return pl.pallas_call(
        functools.partial(_ffn_kernel, n_ff=n_ff, **cfg),
        grid=(n_tok // tm, n_ff),
        in_specs=[
            pl.BlockSpec((tm, D_MODEL), lambda i, k: (i, 0)),
            pl.BlockSpec((None, None, N_MOD, D_MODEL), lambda i, k: (li, row(i), 0, 0)),
            pl.BlockSpec((None, 3, D_MODEL), lambda i, k: (li, 0, 0)),
            pl.BlockSpec((None, None, D_MODEL, tf), lambda i, k: (li, half, 0, k)),
            pl.BlockSpec((None, None, D_MODEL, tf), lambda i, k: (li, half, 0, k)),
            pl.BlockSpec((None, None, tf, D_MODEL), lambda i, k: (li, half, k, 0)),
        ],
        out_specs=[
            pl.BlockSpec((tm, D_MODEL), lambda i, k: (i, 0)),
            pl.BlockSpec((tm, D_MODEL), lambda i, k: (i, 0)),
        ],
        out_shape=[jax.ShapeDtypeStruct((n_tok, D_MODEL), F32),
                   jax.ShapeDtypeStruct((n_tok, D_MODEL), BF16)],
        scratch_shapes=[pltpu.VMEM((tm, D_MODEL), BF16), pltpu.VMEM((tm, D_MODEL), F32)],
        compiler_params=_cparams("parallel", "arbitrary"),
        name=f"ffn_l{li}_h{half}",
    )(x, m_all, norm_g, wg, wu, wd)


def _proj_kernel(a_ref, w_ref, o_ref):
    o_ref[...] = jnp.dot(a_ref[...], w_ref[...], preferred_element_type=F32).astype(o_ref.dtype)


def in_proj(h, w, idx):
    n_tok = h.shape[0]
    n_out = w.shape[-1]
    tm, tn = 1024, 512
    return pl.pallas_call(
        _proj_kernel,
        grid=(n_tok // tm, n_out // tn),
        in_specs=[pl.BlockSpec((tm, D_MODEL), lambda i, j: (i, 0)),
                  pl.BlockSpec((None, D_MODEL, tn), lambda i, j: (idx, 0, j))],
        out_specs=pl.BlockSpec((tm, tn), lambda i, j: (i, j)),
        out_shape=jax.ShapeDtypeStruct((n_tok, n_out), F32),
        compiler_params=_cparams("parallel", "parallel"),
        name="in_proj",
    )(h, w)


def _out_proj_kernel(a_ref, w_ref, x_ref, m_ref, o_ref):
    y = jnp.dot(a_ref[...], w_ref[...], preferred_element_type=F32)
    o_ref[...] = x_ref[...] + m_ref[5:6, :] * y


def out_proj_residual(ycat, w, idx, x, m_all, li, n_prompt_tok, sample_len):
    n_tok = x.shape[0]
    tm = TOKEN_TILE
    row = functools.partial(_cond_row, tm=tm, n_prompt_tok=n_prompt_tok, sample_len=sample_len)
    return pl.pallas_call(
        _out_proj_kernel,
        grid=(n_tok // tm,),
        in_specs=[pl.BlockSpec((tm, D_MODEL), lambda i: (i, 0)),
                  pl.BlockSpec((None, D_MODEL, D_MODEL), lambda i: (idx, 0, 0)),
                  pl.BlockSpec((tm, D_MODEL), lambda i: (i, 0)),
                  pl.BlockSpec((None, None, N_MOD, D_MODEL), lambda i: (li, row(i), 0, 0))],
        out_specs=pl.BlockSpec((tm, D_MODEL), lambda i: (i, 0)),
        out_shape=jax.ShapeDtypeStruct((n_tok, D_MODEL), F32),
        compiler_params=_cparams("parallel"),
        name="out_proj",
    )(ycat, w, x, m_all)


def _rmsnorm(x, g):
    return x * lax.rsqrt(jnp.mean(x * x, axis=-1, keepdims=True) + EPS) * g


def _depthwise_conv(x, w, b, left):
    K = w.shape[0]
    L = x.shape[1]
    xp = jnp.pad(x, ((0, 0), (left, K - 1 - left), (0, 0)))
    y = b + xp[:, 0:L] * w[0]
    for j in range(1, K):
        y = y + xp[:, j:j + L] * w[j]
    return y


def _cmul(ar, ai, br, bi):
    return ar * br - ai * bi, ar * bi + ai * br


def _complex_combine(e1, e2):
    a1r, a1i, b1r, b1i = e1
    a2r, a2i, b2r, b2i = e2
    ar, ai = _cmul(a2r, a2i, a1r, a1i)
    br, bi = _cmul(a2r, a2i, b1r, b1i)
    return ar, ai, br + b2r, bi + b2i


def _real_combine(e1, e2):
    a1, b1 = e1
    a2, b2 = e2
    return a1 * a2, a2 * b1 + b2


def _complex_scan(ar, ai, br, bi, h0r, h0i):
    ir, ii = _cmul(ar, ai, h0r, h0i)
    br = br.at[:, 0].add(ir)
    bi = bi.at[:, 0].add(ii)
    elems = (jnp.broadcast_to(ar, br.shape), jnp.broadcast_to(ai, bi.shape), br, bi)
    _, _, hr, hi = lax.associative_scan(_complex_combine, elems, axis=1)
    return hr, hi


def _real_scan(a, b, h0):
    b = b.at[:, 0].add(a[:, 0] * h0)
    _, h = lax.associative_scan(_real_combine, (a, b), axis=1)
    return h


def s5_mixer(u, lam_re, lam_im, log_dt, b_re, b_im, c_re, c_im, d_skip, glu_w, glu_b, h0, want_state):
    bsz, L, _ = u.shape
    uf = u.reshape(bsz, L, S5_GROUPS, S5_H)
    y = d_skip * uf
    finals = []
    for d in range(2):
        lr = jnp.minimum(lam_re[d], S5_LAM_RE_MAX)
        li = lam_im[d]
        dt = jnp.exp(log_dt[d])[:, None]
        mag = jnp.exp(lr * dt)
        abr, abi = mag * jnp.cos(li * dt), mag * jnp.sin(li * dt)
        den = lr * lr + li * li
        cr = ((abr - 1.0) * lr + abi * li) / den
        ci = (abi * lr - (abr - 1.0) * li) / den
        bbr = cr[..., None] * b_re[d] - ci[..., None] * b_im[d]
        bbi = cr[..., None] * b_im[d] + ci[..., None] * b_re[d]
        ud = uf if d == 0 else uf[:, ::-1]
        bur = jnp.einsum('gph,blgh->blgp', bbr, ud)
        bui = jnp.einsum('gph,blgh->blgp', bbi, ud)
        hr, hi = _complex_scan(abr, abi, bur, bui, h0[:, d, ..., 0], h0[:, d, ..., 1])
        if want_state:
            finals.append(jnp.stack([hr[:, -1], hi[:, -1]], axis=-1))
        if d == 1:
            hr, hi = hr[:, ::-1], hi[:, ::-1]
        y = y + jnp.einsum('ghp,blgp->blgh', c_re[d], hr) - jnp.einsum('ghp,blgp->blgh', c_im[d], hi)
    z = jax.nn.gelu(y.reshape(bsz, L, S5_WIDTH))
    out = z * jax.nn.sigmoid(z @ glu_w + glu_b)
    fin = jnp.stack(finals, axis=1) if want_state else None
    return out, fin


def hyena_filters(L, w1, b1, w2, b2, w3, freq, decay):
    t = jnp.arange(L, dtype=F32)
    t01 = t / L
    bands = jnp.linspace(1e-4, HY_BANDS - 1, HY_BANDS, dtype=F32)
    ang = (2.0 * math.pi / L) * t[:, None] * bands[None, :]
    z = jnp.concatenate([t01[:, None], jnp.cos(ang), -jnp.sin(ang)], axis=-1)
    h = jnp.sin(freq * (z @ w1 + b1))
    h = jnp.sin(freq * (h @ w2 + b2))
    h = (h @ w3).reshape(L, HY_ORDER, 2, HY_WIDTH)
    h = h * jnp.exp(-t01[:, None, None, None] * jnp.abs(decay))
    k = jnp.concatenate([h[:, :, 0], jnp.zeros((1, HY_ORDER, HY_WIDTH), F32), h[:0:-1, :, 1]], axis=0)
    return k / jnp.sum(jnp.abs(k), axis=0, keepdims=True)


def hyena_mixer(p, conv_w, conv_b, w1, b1, w2, b2, w3, freq, decay, bias):
    bsz, L, _ = p.shape
    u = _depthwise_conv(p, conv_w, conv_b, HY_SHORT // 2)
    v, g1, g2 = u[..., :HY_WIDTH], u[..., HY_WIDTH:2 * HY_WIDTH], u[..., 2 * HY_WIDTH:]
    kf = jnp.fft.rfft(hyena_filters(L, w1, b1, w2, b2, w3, freq, decay), axis=0)
    z = v
    for n, gate in enumerate((g1, g2)):
        zf = jnp.fft.rfft(z, n=2 * L, axis=1)
        conv = jnp.fft.irfft(zf * kf[:, n], n=2 * L, axis=1)[:, :L]
        z = gate * (conv + bias[n] * z)
    return z


def rglru_mixer(xb, gb, conv_w, conv_b, w_a, b_a, w_i, b_i, lam, h0, want_state):
    bsz, L, _ = xb.shape
    xc = _depthwise_conv(xb, conv_w, conv_b, RG_CONV // 2)
    hsum = jnp.zeros_like(xc)
    finals = []
    for d in range(2):
        xd = xc if d == 0 else xc[:, ::-1]
        xs = xd.reshape(bsz, L, RG_BLOCKS, RG_BW)
        r = jax.nn.sigmoid(jnp.einsum('blnc,ncd->blnd', xs, w_a[d]) + b_a[d]).reshape(bsz, L, RG_WIDTH)
        i = jax.nn.sigmoid(jnp.einsum('blnc,ncd->blnd', xs, w_i[d]) + b_i[d]).reshape(bsz, L, RG_WIDTH)
        log_a = -RG_C * r * jax.nn.softplus(-lam[d])
        a = jnp.exp(log_a)
        b = jnp.sqrt(-jnp.expm1(2.0 * log_a)) * (i * xd)
        h = _real_scan(a, b, h0[:, d])
        if want_state:
            finals.append(h[:, -1])
        if d == 1:
            h = h[:, ::-1]
        hsum = hsum + h
    y = hsum * jax.nn.gelu(gb)
    fin = jnp.stack(finals, axis=1) if want_state else None
    return y, fin


def ctx_attention(q, k, v):
    bsz, Lc, H, hd = q.shape
    scale = hd ** -0.5
    s = jnp.einsum('bqhd,bkhd->bhqk', q, k) * scale
    p = jax.nn.softmax(s, axis=-1)
    return jnp.einsum('bhqk,bkhd->bqhd', p, v).reshape(bsz, Lc, H * hd)


def na_attention(q, k, v, k_ctx, v_ctx, rpb):
    bsz, L, H, hd = q.shape
    rows = L // GRID_W
    wr = min(NA_WIN_R, rows)
    wc = NA_WIN_C
    scale = hd ** -0.5
    qg = q.reshape(bsz, rows, GRID_W, H, hd)
    kg = k.reshape(bsz, rows, GRID_W, H, hd)
    vg = v.reshape(bsz, rows, GRID_W, H, hd)
    col = jnp.arange(GRID_W)
    col_idx = jnp.clip(col - wc // 2, 0, GRID_W - wc)[:, None] + jnp.arange(wc)[None, :]
    col_bias = rpb[:, :, col_idx - col[:, None] + NA_WIN_C - 1]

    def row_block(r):
        rs = jnp.clip(r - wr // 2, 0, rows - wr)
        kr = lax.dynamic_slice_in_dim(kg, rs, wr, axis=1)[:, :, col_idx]
        vr = lax.dynamic_slice_in_dim(vg, rs, wr, axis=1)[:, :, col_idx]
        qr = lax.dynamic_index_in_dim(qg, r, axis=1, keepdims=False)
        bias = col_bias[:, rs + jnp.arange(wr) - r + NA_WIN_R - 1]
        s_loc = jnp.einsum('bqhd,brqchd->bhqrc', qr, kr) * scale
        s_loc = s_loc + jnp.transpose(bias, (0, 2, 1, 3))[None]
        s_ctx = jnp.einsum('bqhd,bkhd->bhqk', qr, k_ctx) * scale
        s = jnp.concatenate([s_loc.reshape(bsz, H, GRID_W, wr * wc), s_ctx], axis=-1)
        p = jax.nn.softmax(s, axis=-1)
        p_loc = p[..., :wr * wc].reshape(bsz, H, GRID_W, wr, wc)
        return (jnp.einsum('bhqrc,brqchd->bqhd', p_loc, vr)
                + jnp.einsum('bhqk,bkhd->bqhd', p[..., wr * wc:], v_ctx))

    o = lax.map(row_block, jnp.arange(rows))
    return jnp.moveaxis(o, 0, 1).reshape(bsz, L, H * hd)


def kernel(x_prompt, x_sample, cache_na_k, cache_na_v, state_s5, state_rglru, c, c_ctx, norm_g, mod_w, mod_b, ffn_wg, ffn_wu, ffn_wd, even_w_in, even_w_out, s5_lam_re, s5_lam_im, s5_log_dt, s5_b_re, s5_b_im, s5_c_re, s5_c_im, s5_d, s5_glu_w, s5_glu_b, hy_conv_w, hy_conv_b, hy_w1, hy_b1, hy_w2, hy_b2, hy_w3, hy_freq, hy_decay, hy_bias, odd_w_in, odd_w_out, rg_conv_w, rg_conv_b, rg_wa, rg_ba, rg_wi, rg_bi, rg_lam, na_qn, na_kn, na_rpb):
    bp, lp, _ = x_prompt.shape
    bs, ls, _ = x_sample.shape
    n_p = bp * lp
    n_s = bs * ls

    cond = jnp.concatenate([c_ctx[None], c, jnp.zeros((COND_ROWS - 1 - bs, D_MODEL), F32)], axis=0)
    m_all = adaln_all(cond, mod_w, mod_b)

    wg = ffn_wg.astype(BF16)
    wu = ffn_wu.astype(BF16)
    wd = ffn_wd.astype(BF16)
    w_in_e = even_w_in.astype(BF16)
    w_out_e = even_w_out.astype(BF16)
    w_in_o = odd_w_in.astype(BF16)
    w_out_o = odd_w_out.astype(BF16)

    x = jnp.concatenate([x_prompt.reshape(n_p, D_MODEL), x_sample.reshape(n_s, D_MODEL)], axis=0)
    s5_new, rg_new, k_new, v_new = [], [], [], []
    for li in range(DEPTH):
        x, h = ffn_block(x, m_all, norm_g, wg, wu, wd, li, 0, n_p, ls)
        if li % 2 == 0:
            ei = li // 2
            proj = in_proj(h, w_in_e, ei)
            ys = []
            for grp in range(2):
                pg = proj[:n_p].reshape(bp, lp, -1) if grp == 0 else proj[n_p:].reshape(bs, ls, -1)
                h0 = jnp.zeros((bp, 2, S5_GROUPS, S5_P, 2), F32) if grp == 0 else state_s5[:, ei]
                y_a, s5_fin = s5_mixer(pg[..., :S5_WIDTH], s5_lam_re[ei], s5_lam_im[ei], s5_log_dt[ei],
                                       s5_b_re[ei], s5_b_im[ei], s5_c_re[ei], s5_c_im[ei], s5_d[ei],
                                       s5_glu_w[ei], s5_glu_b[ei], h0, grp == 0)
                y_b = hyena_mixer(pg[..., S5_WIDTH:], hy_conv_w[ei], hy_conv_b[ei], hy_w1[ei], hy_b1[ei],
                                  hy_w2[ei], hy_b2[ei], hy_w3[ei], hy_freq[ei], hy_decay[ei], hy_bias[ei])
                if grp == 0:
                    s5_new.append(s5_fin)
                ys.append(jnp.concatenate([y_a, y_b], axis=-1).reshape(-1, D_MODEL))
            ycat = jnp.concatenate(ys, axis=0).astype(BF16)
            x = out_proj_residual(ycat, w_out_e, ei, x, m_all, li, n_p, ls)
        else:
            oi = li // 2
            proj = in_proj(h, w_in_o, oi)
            ys = []
            for grp in range(2):
                pg = proj[:n_p].reshape(bp, lp, -1) if grp == 0 else proj[n_p:].reshape(bs, ls, -1)
                bsz, L = pg.shape[:2]
                h0 = jnp.zeros((bp, 2, RG_WIDTH), F32) if grp == 0 else state_rglru[:, oi]
                y_c, rg_fin = rglru_mixer(pg[..., :RG_WIDTH], pg[..., RG_WIDTH:2 * RG_WIDTH], rg_conv_w[oi],
                                          rg_conv_b[oi], rg_wa[oi], rg_ba[oi], rg_wi[oi], rg_bi[oi], rg_lam[oi],
                                          h0, grp == 0)
                qkv = pg[..., 2 * RG_WIDTH:].reshape(bsz, L, 3, NA_HEADS, NA_HD)
                q = _rmsnorm(qkv[:, :, 0], na_qn[oi])
                k = _rmsnorm(qkv[:, :, 1], na_kn[oi])
                v = qkv[:, :, 2]
                if grp == 0:
                    y_d = ctx_attention(q, k, v)
                    rg_new.append(rg_fin)
                    k_new.append(k)
                    v_new.append(v)
                else:
                    y_d = na_attention(q, k, v, cache_na_k[:, oi], cache_na_v[:, oi], na_rpb[oi])
                ys.append(jnp.concatenate([y_c, y_d], axis=-1).reshape(-1, D_MODEL))
            ycat = jnp.concatenate(ys, axis=0).astype(BF16)
            x = out_proj_residual(ycat, w_out_o, oi, x, m_all, li, n_p, ls)
        x, _ = ffn_block(x, m_all, norm_g, wg, wu, wd, li, 1, n_p, ls)

    y_prompt = x[:n_p].reshape(bp, lp, D_MODEL)
    y_sample = x[n_p:].reshape(bs, ls, D_MODEL)
    new_na_k = jnp.stack(k_new, axis=1)
    new_na_v = jnp.stack(v_new, axis=1)
    new_s5 = jnp.stack(s5_new, axis=1)
    new_rglru = jnp.stack(rg_new, axis=1)
    return (y_prompt, y_sample, new_na_k, new_na_v, new_s5, new_rglru)
```

```python
import functools
import math

import numpy as np
import jax
import jax.numpy as jnp
from jax import lax
from jax.experimental import pallas as pl
from jax.experimental.pallas import tpu as pltpu

F32 = jnp.float32
BF16 = jnp.bfloat16

D_MODEL = 2048
DEPTH = 4
N_MOD = 9
D_FF = 5632
EPS = 1e-6
MIX_HALF = D_MODEL // 2
GRID_W = 64
S5_WIDTH = MIX_HALF
S5_H = 16
S5_GROUPS = S5_WIDTH // S5_H
S5_P = 64
S5_STATE = S5_GROUPS * S5_P
S5_LAM_RE_MAX = -1e-4
HY_WIDTH = MIX_HALF
HY_ORDER = 2
HY_SHORT = 3
HY_BANDS = 16
HY_EMB = 1 + 2 * HY_BANDS
HY_EMB_PAD = 40
HY_FFN = 64
RG_WIDTH = MIX_HALF
RG_BLOCKS = 16
RG_BW = RG_WIDTH // RG_BLOCKS
RG_CONV = 4
RG_LEFT = RG_CONV // 2
RG_C = 8.0
NA_HD = 64
NA_HEADS = MIX_HALF // NA_HD
NA_WIDTH = NA_HEADS * NA_HD
NA_WIN_R = 8
NA_WIN_C = 16

VMEM_LIMIT_BYTES = 56 * 1024 * 1024
LANES = 128
CHAINS = 8
COND_ROWS = 8
TOKEN_TILE = 512
FF_TILE = 512
SCAN_T = 64
S5_COLS = 512
S5_BBLK = 256
S5_CBLK = 128
RG_GBLK = 256
HEAD_CHUNK = 256
NEG_BIG = -1e30
HY_LONG_N2 = 128
HY_DENSE_MAX_L = 512


def _cparams(*sem):
    return pltpu.CompilerParams(dimension_semantics=sem, vmem_limit_bytes=VMEM_LIMIT_BYTES)


def _split_bf16(x):
    hi = x.astype(BF16)
    lo = (x - hi.astype(F32)).astype(BF16)
    return hi, lo


def _dot3(a_hi, a_lo, x):
    x_hi, x_lo = _split_bf16(x)
    return (jnp.dot(a_hi, x_hi, preferred_element_type=F32) + jnp.dot(a_lo, x_hi, preferred_element_type=F32)
            + jnp.dot(a_hi, x_lo, preferred_element_type=F32))


def _dot3_r(x, b_hi, b_lo):
    x_hi, x_lo = _split_bf16(x)
    return (jnp.dot(x_hi, b_hi, preferred_element_type=F32) + jnp.dot(x_hi, b_lo, preferred_element_type=F32)
            + jnp.dot(x_lo, b_hi, preferred_element_type=F32))


def _const_split(m):
    return _split_bf16(jnp.asarray(np.asarray(m, np.float32)))


def _gelu_tanh(x):
    return 0.5 * x * (1.0 + jnp.tanh(math.sqrt(2.0 / math.pi) * (x + 0.044715 * (x * x * x))))


def _adaln_kernel(cond_ref, w_ref, b_ref, o_ref):
    c = cond_ref[...]
    s = (c * jax.nn.sigmoid(c)).astype(BF16)
    o_ref[...] = jnp.dot(s, w_ref[...].astype(BF16), preferred_element_type=F32) + b_ref[...]


def adaln_all(cond, mod_w, mod_b):
    tn = 1024
    n_out = N_MOD * D_MODEL
    out = pl.pallas_call(
        _adaln_kernel,
        grid=(DEPTH, n_out // tn),
        in_specs=[
            pl.BlockSpec((COND_ROWS, D_MODEL), lambda l, j: (0, 0)),
            pl.BlockSpec((None, D_MODEL, tn), lambda l, j: (l, 0, j)),
            pl.BlockSpec((None, 1, tn), lambda l, j: (l, 0, j)),
        ],
        out_specs=pl.BlockSpec((None, COND_ROWS, tn), lambda l, j: (l, 0, j)),
        out_shape=jax.ShapeDtypeStruct((DEPTH, COND_ROWS, n_out), F32),
        compiler_params=_cparams("parallel", "parallel"),
        name="adaln",
    )(cond, mod_w, mod_b.reshape(DEPTH, 1, n_out))
    return out.reshape(DEPTH, COND_ROWS, N_MOD, D_MODEL)


def _rms_modulate(x, g, shift, scale):
    ms = jnp.mean(x * x, axis=-1, keepdims=True)
    return x * lax.rsqrt(ms + EPS) * g * (1.0 + scale) + shift


def _ffn_kernel(x_ref, m_ref, g_ref, wg_ref, wu_ref, wd_ref, o_ref, hn_ref, h_s, acc_s,
                *, n_ff, in_rows, gate_row, g_row, next_rows, next_g_row):
    k = pl.program_id(1)

    @pl.when(k == 0)
    def _():
        h = _rms_modulate(x_ref[...], g_ref[g_row:g_row + 1, :],
                          m_ref[in_rows[0]:in_rows[0] + 1, :], m_ref[in_rows[1]:in_rows[1] + 1, :])
        h_s[...] = h.astype(BF16)
        acc_s[...] = jnp.zeros_like(acc_s)

    h = h_s[...]
    gt = jnp.dot(h, wg_ref[...], preferred_element_type=F32)
    ut = jnp.dot(h, wu_ref[...], preferred_element_type=F32)
    a = (gt * jax.nn.sigmoid(gt) * ut).astype(BF16)
    acc_s[...] += jnp.dot(a, wd_ref[...], preferred_element_type=F32)

    @pl.when(k == n_ff - 1)
    def _():
        xn = x_ref[...] + 0.5 * m_ref[gate_row:gate_row + 1, :] * acc_s[...]
        o_ref[...] = xn
        hn = _rms_modulate(xn, g_ref[next_g_row:next_g_row + 1, :],
                           m_ref[next_rows[0]:next_rows[0] + 1, :], m_ref[next_rows[1]:next_rows[1] + 1, :])
        hn_ref[...] = hn.astype(BF16)


def _cond_row(i, tm, n_prompt_tok, sample_len):
    return jnp.where(i * tm < n_prompt_tok, 0, 1 + (i * tm - n_prompt_tok) // sample_len)


def ffn_block(x, m_all, norm_g, wg, wu, wd, li, half, n_prompt_tok, sample_len):
    n_tok = x.shape[0]
    tm, tf = TOKEN_TILE, FF_TILE
    n_ff = D_FF // tf
    if half == 0:
        cfg = dict(in_rows=(0, 1), gate_row=2, g_row=0, next_rows=(3, 4), next_g_row=1)
    else:
        cfg = dict(in_rows=(6, 7), gate_row=8, g_row=2, next_rows=(3, 4), next_g_row=1)
    row = functools.partial(_cond_row, tm=tm, n_prompt_tok=n_prompt_tok, sample_len=sample_len)
    return pl.pallas_call(
        functools.partial(_ffn_kernel, n_ff=n_ff, **cfg),
        grid=(n_tok // tm, n_ff),
        in_specs=[
            pl.BlockSpec((tm, D_MODEL), lambda i, k: (i, 0)),
            pl.BlockSpec((None, None, N_MOD, D_MODEL), lambda i, k: (li, row(i), 0, 0)),
            pl.BlockSpec((None, 3, D_MODEL), lambda i, k: (li, 0, 0)),
            pl.BlockSpec((None, None, D_MODEL, tf), lambda i, k: (li, half, 0, k)),
            pl.BlockSpec((None, None, D_MODEL, tf), lambda i, k: (li, half, 0, k)),
            pl.BlockSpec((None, None, tf, D_MODEL), lambda i, k: (li, half, k, 0)),
        ],
        out_specs=[
            pl.BlockSpec((tm, D_MODEL), lambda i, k: (i, 0)),
            pl.BlockSpec((tm, D_MODEL), lambda i, k: (i, 0)),
        ],
        out_shape=[jax.ShapeDtypeStruct((n_tok, D_MODEL), F32),
                   jax.ShapeDtypeStruct((n_tok, D_MODEL), BF16)],
        scratch_shapes=[pltpu.VMEM((tm, D_MODEL), BF16), pltpu.VMEM((tm, D_MODEL), F32)],
        compiler_params=_cparams("parallel", "arbitrary"),
        name=f"ffn_l{li}_h{half}",
    )(x, m_all, norm_g, wg, wu, wd)


def _proj_kernel(a_ref, w_ref, o_ref):
    o_ref[...] = jnp.dot(a_ref[...], w_ref[...], preferred_element_type=F32).astype(o_ref.dtype)


def in_proj(h, w, idx):
    n_tok = h.shape[0]
    n_out = w.shape[-1]
    tm, tn = 1024, 512
    return pl.pallas_call(
        _proj_kernel,
        grid=(n_tok // tm, n_out // tn),
        in_specs=[pl.BlockSpec((tm, D_MODEL), lambda i, j: (i, 0)),
                  pl.BlockSpec((None, D_MODEL, tn), lambda i, j: (idx, 0, j))],
        out_specs=pl.BlockSpec((tm, tn), lambda i, j: (i, j)),
        out_shape=jax.ShapeDtypeStruct((n_tok, n_out), F32),
        compiler_params=_cparams("parallel", "parallel"),
        name="in_proj",
    )(h, w)


def _out_proj_kernel(ya_ref, yb_ref, wa_ref, wb_ref, x_ref, m_ref, o_ref):
    y = (jnp.dot(ya_ref[...], wa_ref[...], preferred_element_type=F32)
         + jnp.dot(yb_ref[...], wb_ref[...], preferred_element_type=F32))
    o_ref[...] = x_ref[...] + m_ref[5:6, :] * y


def out_proj_residual(ya, yb, w, idx, x, m_all, li, n_prompt_tok, sample_len):
    n_tok = x.shape[0]
    tm = TOKEN_TILE
    row = functools.partial(_cond_row, tm=tm, n_prompt_tok=n_prompt_tok, sample_len=sample_len)
    return pl.pallas_call(
        _out_proj_kernel,
        grid=(n_tok // tm,),
        in_specs=[pl.BlockSpec((tm, MIX_HALF), lambda i: (i, 0)),
                  pl.BlockSpec((tm, MIX_HALF), lambda i: (i, 0)),
                  pl.BlockSpec((None, MIX_HALF, D_MODEL), lambda i: (idx, 0, 0)),
                  pl.BlockSpec((None, MIX_HALF, D_MODEL), lambda i: (idx, 1, 0)),
                  pl.BlockSpec((tm, D_MODEL), lambda i: (i, 0)),
                  pl.BlockSpec((None, None, N_MOD, D_MODEL), lambda i: (li, row(i), 0, 0))],
        out_specs=pl.BlockSpec((tm, D_MODEL), lambda i: (i, 0)),
        out_shape=jax.ShapeDtypeStruct((n_tok, D_MODEL), F32),
        compiler_params=_cparams("parallel"),
        name="out_proj",
    )(ya, yb, w, w, x, m_all)


def _s5_scan_kernel(u_ref, bw_ref, a_ref, cw_ref, h0_ref, y_ref, hfin_ref, br_s, bi_s, hr_s, hi_s, *, n_t, t_tile):
    d = pl.program_id(0)
    jt = pl.program_id(2)
    rows = t_tile * CHAINS

    @pl.when(jt == 0)
    def _():
        hr_s[...] = h0_ref[:, :S5_STATE]
        hi_s[...] = h0_ref[:, S5_STATE:]

    u = u_ref[...].reshape(rows, S5_WIDTH).astype(BF16)
    n_bblk = S5_WIDTH // S5_BBLK
    ncol = S5_STATE // n_bblk
    for blk in range(n_bblk):
        bu = jnp.dot(u[:, blk * S5_BBLK:(blk + 1) * S5_BBLK], bw_ref[blk], preferred_element_type=F32)
        br_s[:, blk * ncol:(blk + 1) * ncol] = bu[:, :ncol]
        bi_s[:, blk * ncol:(blk + 1) * ncol] = bu[:, ncol:]

    for cc in range(S5_STATE // S5_COLS):
        cs = slice(cc * S5_COLS, (cc + 1) * S5_COLS)
        ar = a_ref[0, :, cs]
        ai = a_ref[1, :, cs]

        def step(i, carry):
            hr, hi = carry
            t = jnp.where(d == 0, i, t_tile - 1 - i)
            r0 = pl.multiple_of(t * CHAINS, CHAINS)
            nhr = ar * hr - ai * hi + br_s[pl.ds(r0, CHAINS), cs]
            nhi = ar * hi + ai * hr + bi_s[pl.ds(r0, CHAINS), cs]
            br_s[pl.ds(r0, CHAINS), cs] = nhr
            bi_s[pl.ds(r0, CHAINS), cs] = nhi
            return nhr, nhi

        hr, hi = lax.fori_loop(0, t_tile, step, (hr_s[:, cs], hi_s[:, cs]), unroll=4)
        hr_s[:, cs] = hr
        hi_s[:, cs] = hi

    n_cblk = S5_WIDTH // S5_CBLK
    kc = S5_STATE // n_cblk
    for i in range(n_cblk):
        yi = jnp.dot(br_s[:, i * kc:(i + 1) * kc].astype(BF16), cw_ref[0, i], preferred_element_type=F32)
        yi = yi + jnp.dot(bi_s[:, i * kc:(i + 1) * kc].astype(BF16), cw_ref[1, i], preferred_element_type=F32)
        y_ref[:, :, i * S5_CBLK:(i + 1) * S5_CBLK] = yi.reshape(t_tile, CHAINS, S5_CBLK)

    @pl.when(jt == n_t - 1)
    def _():
        hfin_ref[:, :S5_STATE] = hr_s[...]
        hfin_ref[:, S5_STATE:] = hi_s[...]


def s5_scan(u_tm, bw, a_b, cw, h0):
    L, nc, _ = u_tm.shape
    t_tile = min(SCAN_T, L)
    n_t = L // t_tile
    tix = lambda d, j: jnp.where(d == 0, j, n_t - 1 - j)
    return pl.pallas_call(
        functools.partial(_s5_scan_kernel, n_t=n_t, t_tile=t_tile),
        grid=(2, nc // CHAINS, n_t),
        in_specs=[
            pl.BlockSpec((t_tile, CHAINS, S5_WIDTH), lambda d, c, j: (tix(d, j), c, 0)),
            pl.BlockSpec((None,) + bw.shape[1:], lambda d, c, j: (d, 0, 0, 0)),
            pl.BlockSpec((None,) + a_b.shape[1:], lambda d, c, j: (d, 0, 0, 0)),
            pl.BlockSpec((None,) + cw.shape[1:], lambda d, c, j: (d, 0, 0, 0, 0)),
            pl.BlockSpec((None, CHAINS, 2 * S5_STATE), lambda d, c, j: (d, c, 0)),
        ],
        out_specs=[
            pl.BlockSpec((None, t_tile, CHAINS, S5_WIDTH), lambda d, c, j: (d, tix(d, j), c, 0)),
            pl.BlockSpec((None, CHAINS, 2 * S5_STATE), lambda d, c, j: (d, c, 0)),
        ],
        out_shape=[jax.ShapeDtypeStruct((2, L, nc, S5_WIDTH), F32),
                   jax.ShapeDtypeStruct((2, nc, 2 * S5_STATE), F32)],
        scratch_shapes=[pltpu.VMEM((t_tile * CHAINS, S5_STATE), F32), pltpu.VMEM((t_tile * CHAINS, S5_STATE), F32),
                        pltpu.VMEM((CHAINS, S5_STATE), F32), pltpu.VMEM((CHAINS, S5_STATE), F32)],
        compiler_params=_cparams("parallel", "parallel", "arbitrary"),
        name="s5_scan",
    )(u_tm, bw, a_b, cw, h0)


def s5_params(lam_re, lam_im, log_dt, b_re, b_im, c_re, c_im):
    lr = jnp.minimum(lam_re, S5_LAM_RE_MAX)
    li = lam_im
    dt = jnp.exp(log_dt)[..., None]
    mag = jnp.exp(lr * dt)
    abr, abi = mag * jnp.cos(li * dt), mag * jnp.sin(li * dt)
    den = lr * lr + li * li
    cr = ((abr - 1.0) * lr + abi * li) / den
    ci = (abi * lr - (abr - 1.0) * li) / den
    bbr = cr[..., None] * b_re - ci[..., None] * b_im
    bbi = cr[..., None] * b_im + ci[..., None] * b_re
    n_bblk = S5_WIDTH // S5_BBLK
    gb = S5_GROUPS // n_bblk
    bb = jnp.stack([bbr, bbi], axis=1).reshape(2, 2, n_bblk, gb, S5_P, S5_H)
    bw = jnp.einsum('drbgph,gk->dbghrkp', bb, jnp.eye(gb, dtype=F32)).reshape(2, n_bblk, gb * S5_H, 2 * gb * S5_P)
    n_cblk = S5_WIDTH // S5_CBLK
    gc = S5_GROUPS // n_cblk
    cc = jnp.stack([c_re, -c_im], axis=1).reshape(2, 2, n_cblk, gc, S5_H, S5_P)
    cw = jnp.einsum('drbghp,gk->drbgpkh', cc, jnp.eye(gc, dtype=F32)).reshape(2, 2, n_cblk, gc * S5_P, gc * S5_H)
    a_b = jnp.stack([abr.reshape(2, S5_STATE), abi.reshape(2, S5_STATE)], axis=1)
    a_b = jnp.broadcast_to(a_b[:, :, None, :], (2, 2, CHAINS, S5_STATE))
    return bw.astype(BF16), a_b, cw.astype(BF16)


def _s5_glu_kernel(u_ref, y_ref, dsk_ref, w_ref, b_ref, o_ref):
    y = dsk_ref[...] * u_ref[...] + y_ref[0] + y_ref[1]
    z = _gelu_tanh(y)
    gate = jnp.dot(z.astype(BF16), w_ref[...], preferred_element_type=F32) + b_ref[...]
    o_ref[...] = (z * jax.nn.sigmoid(gate)).astype(o_ref.dtype)


def s5_glu(u2d, y2d, d_skip, glu_w, glu_b):
    R = u2d.shape[0]
    tm = TOKEN_TILE
    return pl.pallas_call(
        _s5_glu_kernel,
        grid=(R // tm,),
        in_specs=[pl.BlockSpec((tm, S5_WIDTH), lambda i: (i, 0)),
                  pl.BlockSpec((2, tm, S5_WIDTH), lambda i: (0, i, 0)),
                  pl.BlockSpec((1, S5_WIDTH), lambda i: (0, 0)),
                  pl.BlockSpec((S5_WIDTH, S5_WIDTH), lambda i: (0, 0)),
                  pl.BlockSpec((1, S5_WIDTH), lambda i: (0, 0))],
        out_specs=pl.BlockSpec((tm, S5_WIDTH), lambda i: (i, 0)),
        out_shape=jax.ShapeDtypeStruct((R, S5_WIDTH), BF16),
        compiler_params=_cparams("parallel"),
        name="s5_glu",
    )(u2d, y2d, d_skip.reshape(1, S5_WIDTH), glu_w.astype(BF16), glu_b.reshape(1, S5_WIDTH))


def _rglru_kernel(xp_ref, xc_ref, xn_ref, cw_ref, cb_ref, wa_ref, ba_ref, wi_ref, bi_ref, lam_ref, h0_ref,
                  h_ref, hfin_ref, a_s, b_s, hc_s, *, n_t, t_tile):
    d = pl.program_id(0)
    jt = pl.program_id(2)
    tj = jnp.where(d == 0, jt, n_t - 1 - jt)
    rows = t_tile * CHAINS

    @pl.when(jt == 0)
    def _():
        hc_s[...] = h0_ref[...]

    prev = xp_ref[...] * (tj > 0).astype(F32)
    nxt = xn_ref[...] * (tj < n_t - 1).astype(F32)
    xcat = jnp.concatenate([prev, xc_ref[...], nxt], axis=0)
    xc = cb_ref[...] + cw_ref[0:1, :] * xcat[0:t_tile]
    for j in range(1, RG_CONV):
        xc = xc + cw_ref[j:j + 1, :] * xcat[j:j + t_tile]
    x2 = xc.reshape(rows, RG_WIDTH)
    xb = x2.astype(BF16)

    ra, ia = [], []
    for j in range(RG_WIDTH // RG_GBLK):
        xs = xb[:, j * RG_GBLK:(j + 1) * RG_GBLK]
        ra.append(jnp.dot(xs, wa_ref[j], preferred_element_type=F32))
        ia.append(jnp.dot(xs, wi_ref[j], preferred_element_type=F32))
    r = jax.nn.sigmoid(jnp.concatenate(ra, axis=1) + ba_ref[...])
    ig = jax.nn.sigmoid(jnp.concatenate(ia, axis=1) + bi_ref[...])
    nl = -lam_ref[...]
    sp = jnp.maximum(nl, 0.0) + jnp.log1p(jnp.exp(-jnp.abs(nl)))
    log_a = (-RG_C) * r * sp
    a_s[...] = jnp.exp(log_a)
    th = jnp.tanh(log_a)
    one_minus_a2 = (-2.0) * th / (1.0 - th)
    b_s[...] = jnp.sqrt(one_minus_a2) * (ig * x2)

    def step(i, h):
        t = jnp.where(d == 0, i, t_tile - 1 - i)
        r0 = pl.multiple_of(t * CHAINS, CHAINS)
        h = a_s[pl.ds(r0, CHAINS), :] * h + b_s[pl.ds(r0, CHAINS), :]
        h_ref[t] = h
        return h

    h = lax.fori_loop(0, t_tile, step, hc_s[...], unroll=8)
    hc_s[...] = h

    @pl.when(jt == n_t - 1)
    def _():
        hfin_ref[...] = h


def rglru_scan(x_tm, col_blk, conv_w, conv_b, wa, ba, wi, bi, lam, h0):
    L, nc, _ = x_tm.shape
    t_tile = min(SCAN_T, L)
    n_t = L // t_tile
    tix = lambda d, j: jnp.where(d == 0, j, n_t - 1 - j)
    half = t_tile // RG_LEFT
    return pl.pallas_call(
        functools.partial(_rglru_kernel, n_t=n_t, t_tile=t_tile),
        grid=(2, nc // CHAINS, n_t),
        in_specs=[
            pl.BlockSpec((RG_LEFT, CHAINS, RG_WIDTH), lambda d, c, j: (jnp.maximum(tix(d, j) * half - 1, 0), c, col_blk)),
            pl.BlockSpec((t_tile, CHAINS, RG_WIDTH), lambda d, c, j: (tix(d, j), c, col_blk)),
            pl.BlockSpec((1, CHAINS, RG_WIDTH), lambda d, c, j: (jnp.minimum((tix(d, j) + 1) * t_tile, L - 1), c, col_blk)),
            pl.BlockSpec((RG_CONV, RG_WIDTH), lambda d, c, j: (0, 0)),
            pl.BlockSpec((1, RG_WIDTH), lambda d, c, j: (0, 0)),
            pl.BlockSpec((None,) + wa.shape[1:], lambda d, c, j: (d, 0, 0, 0)),
            pl.BlockSpec((None, 1, RG_WIDTH), lambda d, c, j: (d, 0, 0)),
            pl.BlockSpec((None,) + wi.shape[1:], lambda d, c, j: (d, 0, 0, 0)),
            pl.BlockSpec((None, 1, RG_WIDTH), lambda d, c, j: (d, 0, 0)),
            pl.BlockSpec((None, 1, RG_WIDTH), lambda d, c, j: (d, 0, 0)),
            pl.BlockSpec((None, CHAINS, RG_WIDTH), lambda d, c, j: (d, c, 0)),
        ],
        out_specs=[
            pl.BlockSpec((None, t_tile, CHAINS, RG_WIDTH), lambda d, c, j: (d, tix(d, j), c, 0)),
            pl.BlockSpec((None, CHAINS, RG_WIDTH), lambda d, c, j: (d, c, 0)),
        ],
        out_shape=[jax.ShapeDtypeStruct((2, L, nc, RG_WIDTH), F32),
                   jax.ShapeDtypeStruct((2, nc, RG_WIDTH), F32)],
        scratch_shapes=[pltpu.VMEM((t_tile * CHAINS, RG_WIDTH), F32), pltpu.VMEM((t_tile * CHAINS, RG_WIDTH), F32),
                        pltpu.VMEM((CHAINS, RG_WIDTH), F32)],
        compiler_params=_cparams("parallel", "parallel", "arbitrary"),
        name="rglru_scan",
    )(x_tm, x_tm, x_tm, conv_w, conv_b.reshape(1, RG_WIDTH), wa, ba, wi, bi, lam.reshape(2, 1, RG_WIDTH), h0)


def rglru_params(w_a, b_a, w_i, b_i):
    nb = RG_WIDTH // RG_GBLK
    per = RG_GBLK // RG_BW
    eye = jnp.eye(per, dtype=F32)

    def bd(w):
        w = w.reshape(2, nb, per, RG_BW, RG_BW)
        return jnp.einsum('dbnce,nm->dbncme', w, eye).reshape(2, nb, RG_GBLK, RG_GBLK).astype(BF16)

    return bd(w_a), b_a.reshape(2, 1, RG_WIDTH), bd(w_i), b_i.reshape(2, 1, RG_WIDTH)


def _rg_combine_kernel(h_ref, g_ref, o_ref):
    o_ref[...] = ((h_ref[0] + h_ref[1]) * _gelu_tanh(g_ref[...])).astype(o_ref.dtype)


def rg_combine(h2, g_src, col_blk):
    R = h2.shape[1]
    tm = TOKEN_TILE
    return pl.pallas_call(
        _rg_combine_kernel,
        grid=(R // tm,),
        in_specs=[pl.BlockSpec((2, tm, RG_WIDTH), lambda i: (0, i, 0)),
                  pl.BlockSpec((tm, RG_WIDTH), lambda i: (i, col_blk))],
        out_specs=pl.BlockSpec((tm, RG_WIDTH), lambda i: (i, 0)),
        out_shape=jax.ShapeDtypeStruct((R, RG_WIDTH), BF16),
        compiler_params=_cparams("parallel"),
        name="rg_combine",
    )(h2, g_src)


def _head_rms(x, ones_bd, gain):
    sq = x * x
    hi, lo = _split_bf16(sq)
    parts = []
    for j in range(NA_WIDTH // HEAD_CHUNK):
        cs = slice(j * HEAD_CHUNK, (j + 1) * HEAD_CHUNK)
        parts.append(jnp.dot(hi[:, cs], ones_bd, preferred_element_type=F32)
                     + jnp.dot(lo[:, cs], ones_bd, preferred_element_type=F32))
    ms = jnp.concatenate(parts, axis=1) * (1.0 / NA_HD)
    return x * lax.rsqrt(ms + EPS) * gain


def _qkv_prep_kernel(q_ref, k_ref, v_ref, ones_ref, gq_ref, gk_ref, qo_ref, ko_ref, vo_ref, kf_ref):
    ones_bd = ones_ref[...]
    qo_ref[...] = _head_rms(q_ref[...], ones_bd, gq_ref[...]).astype(BF16)
    kn = _head_rms(k_ref[...], ones_bd, gk_ref[...])
    ko_ref[...] = kn.astype(BF16)
    kf_ref[...] = kn
    vo_ref[...] = v_ref[...].astype(BF16)


def qkv_prep(proj, col0, qn_g, kn_g):
    n_tok = proj.shape[0]
    tm = TOKEN_TILE
    ones_bd = jnp.kron(jnp.eye(HEAD_CHUNK // NA_HD, dtype=F32), jnp.ones((NA_HD, NA_HD), F32)).astype(BF16)
    gq = jnp.tile(qn_g, NA_HEADS).reshape(1, NA_WIDTH)
    gk = jnp.tile(kn_g, NA_HEADS).reshape(1, NA_WIDTH)
    blk = lambda c: pl.BlockSpec((tm, NA_WIDTH), lambda i: (i, c))
    full = lambda s: pl.BlockSpec(s, lambda i: (0, 0))
    return pl.pallas_call(
        _qkv_prep_kernel,
        grid=(n_tok // tm,),
        in_specs=[blk(col0), blk(col0 + 1), blk(col0 + 2), full((HEAD_CHUNK, HEAD_CHUNK)),
                  full((1, NA_WIDTH)), full((1, NA_WIDTH))],
        out_specs=[blk(0), blk(0), blk(0), blk(0)],
        out_shape=[jax.ShapeDtypeStruct((n_tok, NA_WIDTH), BF16)] * 3 + [jax.ShapeDtypeStruct((n_tok, NA_WIDTH), F32)],
        compiler_params=_cparams("parallel"),
        name="qkv_prep",
    )(proj, proj, proj, ones_bd, gq, gk)


_NT = (((1,), (1,)), ((), ()))


def _ctx_attn_kernel(q_ref, k_ref, v_ref, o_ref):
    scale = NA_HD ** -0.5
    outs = []
    for h in range(NA_HEADS):
        hs = slice(h * NA_HD, (h + 1) * NA_HD)
        s = lax.dot_general(q_ref[:, hs], k_ref[:, hs], _NT, preferred_element_type=F32) * scale
        m = jnp.max(s, axis=-1, keepdims=True)
        e = jnp.exp(s - m)
        l = jnp.sum(e, axis=-1, keepdims=True)
        o = jnp.dot(e.astype(BF16), v_ref[:, hs], preferred_element_type=F32)
        outs.append(o / l)
        if h % 2 == 1:
            o_ref[:, (h - 1) * NA_HD:(h + 1) * NA_HD] = jnp.concatenate(outs, axis=1).astype(o_ref.dtype)
            outs = []


def ctx_attention(q, k, v, n_seq, seq_len):
    blk = pl.BlockSpec((seq_len, NA_WIDTH), lambda b: (b, 0))
    return pl.pallas_call(
        _ctx_attn_kernel,
        grid=(n_seq,),
        in_specs=[blk, blk, blk],
        out_specs=blk,
        out_shape=jax.ShapeDtypeStruct((n_seq * seq_len, NA_WIDTH), BF16),
        compiler_params=_cparams("parallel"),
        name="ctx_attn",
    )(q, k, v)


def _na_attn_kernel(q_ref, k_ref, v_ref, kc_ref, vc_ref, bias_ref, o_ref, *, rows):
    r = pl.program_id(1)
    rs = jnp.clip(r - NA_WIN_R // 2, 0, rows - NA_WIN_R)
    k0 = pl.multiple_of(rs * GRID_W, GRID_W)
    nloc = NA_WIN_R * GRID_W
    scale = NA_HD ** -0.5
    outs = []
    for h in range(NA_HEADS):
        hs = slice(h * NA_HD, (h + 1) * NA_HD)
        qh = q_ref[:, hs]
        s_loc = lax.dot_general(qh, k_ref[pl.ds(k0, nloc), hs], _NT, preferred_element_type=F32) * scale + bias_ref[h]
        s_ctx = lax.dot_general(qh, kc_ref[:, hs], _NT, preferred_element_type=F32) * scale
        m = jnp.maximum(jnp.max(s_loc, axis=-1, keepdims=True), jnp.max(s_ctx, axis=-1, keepdims=True))
        e_loc = jnp.exp(s_loc - m)
        e_ctx = jnp.exp(s_ctx - m)
        l = jnp.sum(e_loc, axis=-1, keepdims=True) + jnp.sum(e_ctx, axis=-1, keepdims=True)
        o = (jnp.dot(e_loc.astype(BF16), v_ref[pl.ds(k0, nloc), hs], preferred_element_type=F32)
             + jnp.dot(e_ctx.astype(BF16), vc_ref[:, hs], preferred_element_type=F32))
        outs.append(o / l)
        if h % 2 == 1:
            o_ref[:, (h - 1) * NA_HD:(h + 1) * NA_HD] = jnp.concatenate(outs, axis=1).astype(o_ref.dtype)
            outs = []


def na_bias_table(rpb):
    qc = jnp.arange(GRID_W)
    kc = jnp.arange(GRID_W)
    cs = jnp.clip(qc - NA_WIN_C // 2, 0, GRID_W - NA_WIN_C)
    valid = (kc[None, :] >= cs[:, None]) & (kc[None, :] < cs[:, None] + NA_WIN_C)
    colrel = jnp.clip(kc[None, :] - qc[:, None] + NA_WIN_C - 1, 0, 2 * NA_WIN_C - 2)
    cls = jnp.arange(NA_WIN_R)
    j = jnp.arange(NA_WIN_R)
    rowrel = j[None, :] - cls[:, None] + NA_WIN_R - 1
    tab = rpb[:, rowrel][:, :, :, colrel]
    tab = jnp.where(valid[None, None, None], tab, NEG_BIG)
    tab = jnp.transpose(tab, (0, 1, 3, 2, 4))
    return tab.reshape(rpb.shape[0], NA_WIN_R, GRID_W, NA_WIN_R * GRID_W)


def na_attention(q, k, v, k_ctx, v_ctx, bias_tab, tok0, n_seq, seq_len):
    rows = seq_len // GRID_W
    past = k_ctx.shape[1]
    assert tok0 % seq_len == 0 and rows >= NA_WIN_R
    seq0 = tok0 // seq_len
    row0 = tok0 // GRID_W
    cls = lambda r: r - jnp.clip(r - NA_WIN_R // 2, 0, rows - NA_WIN_R)
    return pl.pallas_call(
        functools.partial(_na_attn_kernel, rows=rows),
        grid=(n_seq, rows),
        in_specs=[pl.BlockSpec((GRID_W, NA_WIDTH), lambda b, r: (row0 + b * rows + r, 0)),
                  pl.BlockSpec((seq_len, NA_WIDTH), lambda b, r: (seq0 + b, 0)),
                  pl.BlockSpec((seq_len, NA_WIDTH), lambda b, r: (seq0 + b, 0)),
                  pl.BlockSpec((None, past, NA_WIDTH), lambda b, r: (b, 0, 0)),
                  pl.BlockSpec((None, past, NA_WIDTH), lambda b, r: (b, 0, 0)),
                  pl.BlockSpec((NA_HEADS, None, GRID_W, NA_WIN_R * GRID_W), lambda b, r: (0, cls(r), 0, 0))],
        out_specs=pl.BlockSpec((GRID_W, NA_WIDTH), lambda b, r: (b * rows + r, 0)),
        out_shape=jax.ShapeDtypeStruct((n_seq * seq_len, NA_WIDTH), BF16),
        compiler_params=_cparams("parallel", "arbitrary"),
        name="na_attn",
    )(q, k, v, k_ctx, v_ctx, bias_tab)


class HyPlan:
    def __init__(self, L, n2):
        self.L, self.N, self.N2 = L, 2 * L, n2
        self.N1 = self.N // n2
        self.H1 = self.N1 // 2
        self.K1 = self.N1 // 2 + 1
        self.K1P = -(-self.K1 // 8) * 8
        N1, H1, K1, K1P, N, N2 = self.N1, self.H1, self.K1, self.K1P, self.N, self.N2
        k1 = np.arange(K1)[:, None]
        n1 = np.arange(H1)[None, :]
        ang = 2.0 * np.pi * ((k1 * n1) % N1) / N1
        ma = np.zeros((2 * K1P, H1))
        ma[:K1] = np.cos(ang)
        ma[K1P:K1P + K1] = -np.sin(ang)
        self.ma = ma
        w = np.where((np.arange(K1) == 0) | (np.arange(K1) == N1 // 2), 1.0, 2.0)[None, :]
        angi = 2.0 * np.pi * ((np.arange(H1)[:, None] * np.arange(K1)[None, :]) % N1) / N1
        g = np.zeros((H1, 2 * K1P))
        g[:, :K1] = w * np.cos(angi) / N
        g[:, K1P:K1P + K1] = -w * np.sin(angi) / N
        self.g = g
        if N2 > 1:
            a2 = 2.0 * np.pi * ((np.arange(N2)[:, None] * np.arange(N2)[None, :]) % N2) / N2
            self.w2 = np.concatenate([np.cos(a2), -np.sin(a2)], axis=0)
            at = 2.0 * np.pi * ((np.arange(K1)[:, None] * np.arange(N2)[None, :]) % N) / N
            tw = np.stack([np.cos(at), -np.sin(at)], axis=1)
            self.tw = np.broadcast_to(tw[..., None], (K1, 2, N2, LANES)).astype(np.float32)


def _conv3_kernel(x_ref, w_ref, b_ref, o_ref):
    x = x_ref[...]
    L = x.shape[0]
    row = lax.broadcasted_iota(jnp.int32, x.shape, 0)
    xm = jnp.where(row == 0, 0.0, pltpu.roll(x, 1, axis=0))
    xp = jnp.where(row == L - 1, 0.0, pltpu.roll(x, L - 1, axis=0))
    o_ref[...] = b_ref[...] + w_ref[0:1, :] * xm + w_ref[1:2, :] * x + w_ref[2:3, :] * xp


def hy_conv3(proj3, col0, conv_w, conv_b, width):
    B, L, _ = proj3.shape
    ct = 256
    nj = width // ct
    return pl.pallas_call(
        _conv3_kernel,
        grid=(B, 3, nj),
        in_specs=[pl.BlockSpec((None, L, ct), lambda b, s, j: (b, 0, (col0 + s) * nj + j)),
                  pl.BlockSpec((HY_SHORT, ct), lambda b, s, j: (0, s * nj + j)),
                  pl.BlockSpec((1, ct), lambda b, s, j: (0, s * nj + j))],
        out_specs=pl.BlockSpec((None, None, L, ct), lambda b, s, j: (s, b, 0, j)),
        out_shape=jax.ShapeDtypeStruct((3, B, L, width), F32),
        compiler_params=_cparams("parallel", "parallel", "parallel"),
        name="hy_conv3",
    )(proj3, conv_w, conv_b.reshape(1, -1))


def _hy_filter_kernel(z_ref, t_ref, w1_ref, b1_ref, w2_ref, b2_ref, fr_ref, w3f_ref, w3b_ref, dec_ref, o_ref):
    fr = fr_ref[...]
    h = jnp.sin(fr * (_dot3_r(z_ref[...], *_split_bf16(w1_ref[...])) + b1_ref[...]))
    h = jnp.sin(fr * (_dot3_r(h, *_split_bf16(w2_ref[...])) + b2_ref[...]))
    h_hi, h_lo = _split_bf16(h)
    ct = w3f_ref.shape[1]
    t01 = jnp.concatenate([t_ref[...]] * (ct // LANES), axis=1)
    row = lax.broadcasted_iota(jnp.int32, t01.shape, 0)

    def branch(w3_ref, dec):
        w_hi, w_lo = _split_bf16(w3_ref[...])
        hv = (jnp.dot(h_hi, w_hi, preferred_element_type=F32) + jnp.dot(h_hi, w_lo, preferred_element_type=F32)
              + jnp.dot(h_lo, w_hi, preferred_element_type=F32))
        return hv * jnp.exp(-t01 * jnp.abs(dec))

    hf = branch(w3f_ref, dec_ref[0])
    hb = jnp.where(row == 0, 0.0, branch(w3b_ref, dec_ref[1]))
    norm = jnp.sum(jnp.abs(hf), axis=0, keepdims=True) + jnp.sum(jnp.abs(hb), axis=0, keepdims=True)
    inv = 1.0 / norm
    o_ref[0] = hf * inv
    o_ref[1] = hb * inv


def hy_filters(L, w1, b1, w2, b2, w3, freq, decay, width):
    t = jnp.arange(L, dtype=F32)
    t01 = t / L
    bands = jnp.linspace(1e-4, HY_BANDS - 1, HY_BANDS, dtype=F32)
    ang = (2.0 * math.pi / L) * t[:, None] * bands[None, :]
    z = jnp.concatenate([t01[:, None], jnp.cos(ang), -jnp.sin(ang), jnp.zeros((L, HY_EMB_PAD - HY_EMB), F32)], axis=-1)
    w1p = jnp.concatenate([w1, jnp.zeros((HY_EMB_PAD - HY_EMB, HY_FFN), F32)], axis=0)
    ct = 256
    nj = width // ct
    full = lambda s: pl.BlockSpec(s, lambda o, j: (0,) * len(s))
    return pl.pallas_call(
        _hy_filter_kernel,
        grid=(HY_ORDER, nj),
        in_specs=[full((L, HY_EMB_PAD)), full((L, LANES)), full((HY_EMB_PAD, HY_FFN)), full((1, HY_FFN)),
                  full((HY_FFN, HY_FFN)), full((1, HY_FFN)), full((1, HY_FFN)),
                  pl.BlockSpec((HY_FFN, ct), lambda o, j: (0, (2 * o) * nj + j)),
                  pl.BlockSpec((HY_FFN, ct), lambda o, j: (0, (2 * o + 1) * nj + j)),
                  pl.BlockSpec((None, 2, 1, ct), lambda o, j: (o, 0, 0, j))],
        out_specs=pl.BlockSpec((None, 2, L, ct), lambda o, j: (o, 0, 0, j)),
        out_shape=jax.ShapeDtypeStruct((HY_ORDER, 2, L, width), F32),
        compiler_params=_cparams("parallel", "parallel"),
        name="hy_filter",
    )(z, jnp.broadcast_to(t01[:, None], (L, LANES)), w1p, b1.reshape(1, -1), w2, b2.reshape(1, -1),
      freq.reshape(1, -1), w3, w3, decay.reshape(HY_ORDER, 2, 1, width))


def _stage_a_kernel(x_ref, mhi_ref, mlo_ref, o_ref):
    o_ref[...] = _dot3(mhi_ref[...], mlo_ref[...], x_ref[...])


def hy_stage_a(x, plan):
    B, L, C = x.shape
    cols = plan.N2 * C
    xv = x.reshape(B, plan.H1, cols)
    tc = min(cols, 8192)
    mhi, mlo = _const_split(plan.ma)
    return pl.pallas_call(
        _stage_a_kernel,
        grid=(B, cols // tc),
        in_specs=[pl.BlockSpec((None, plan.H1, tc), lambda b, j: (b, 0, j)),
                  pl.BlockSpec(mhi.shape, lambda b, j: (0, 0)),
                  pl.BlockSpec(mlo.shape, lambda b, j: (0, 0))],
        out_specs=pl.BlockSpec((None, 2 * plan.K1P, tc), lambda b, j: (b, 0, j)),
        out_shape=jax.ShapeDtypeStruct((B, 2 * plan.K1P, cols), F32),
        compiler_params=_cparams("parallel", "parallel"),
        name="hy_stage_a",
    )(xv, mhi, mlo)


def _cplx_left(w_hi, w_lo, xr, xi, n2, conj):
    p = _dot3(w_hi, w_lo, jnp.concatenate([xr, xi], axis=1))
    rr, ri = p[:n2, :LANES], p[:n2, LANES:]
    ir, ii = p[n2:, :LANES], p[n2:, LANES:]
    if conj:
        return rr + ii, ri - ir
    return rr - ii, ri + ir


def _stage_b_body(ar_ref, ai_ref, tw_ref, whi_ref, wlo_ref, rest, n2, ct, with_filter):
    if with_filter:
        kfr_ref, kfi_ref, kbr_ref, kbi_ref, or_ref, oi_ref = rest
    else:
        or_ref, oi_ref = rest
    tr, ti = tw_ref[0], tw_ref[1]
    w_hi, w_lo = whi_ref[...], wlo_ref[...]
    for c in range(ct // LANES):
        cs = slice(c * LANES, (c + 1) * LANES)
        ar, ai = ar_ref[:, cs], ai_ref[:, cs]
        xr, xi = _cplx_left(w_hi, w_lo, ar * tr - ai * ti, ar * ti + ai * tr, n2, False)
        if not with_filter:
            or_ref[:, cs] = xr
            oi_ref[:, cs] = xi
            continue
        kr = kfr_ref[:, cs] + kbr_ref[:, cs]
        ki = kfi_ref[:, cs] - kbi_ref[:, cs]
        cr, ci = _cplx_left(w_hi, w_lo, xr * kr - xi * ki, xr * ki + xi * kr, n2, True)
        or_ref[:, cs] = cr * tr + ci * ti
        oi_ref[:, cs] = ci * tr - cr * ti


def _stage_b_kernel(ar_ref, ai_ref, tw_ref, whi_ref, wlo_ref, *rest, n2, ct, k1n, with_filter):
    or_ref, oi_ref = rest[-2:]

    @pl.when(pl.program_id(1) >= k1n)
    def _():
        or_ref[...] = jnp.zeros_like(or_ref)
        oi_ref[...] = jnp.zeros_like(oi_ref)

    @pl.when(pl.program_id(1) < k1n)
    def _():
        _stage_b_body(ar_ref, ai_ref, tw_ref, whi_ref, wlo_ref, rest, n2, ct, with_filter)


def hy_stage_b(a, plan, C, kx=None, order=0):
    B = a.shape[0]
    n2, k1n, k1p = plan.N2, plan.K1, plan.K1P
    ct = 512 if C % 512 == 0 else C
    nj = C // ct
    a4 = a.reshape(B, 2 * k1p, n2, C)
    w_hi, w_lo = _const_split(plan.w2)
    blk = lambda off: pl.BlockSpec((None, None, n2, ct), lambda b, k, j: (b, off + k, 0, j))
    in_specs = [blk(0), blk(k1p),
                pl.BlockSpec((None, 2, n2, LANES), lambda b, k, j: (jnp.minimum(k, k1n - 1), 0, 0, 0)),
                pl.BlockSpec(w_hi.shape, lambda b, k, j: (0, 0)),
                pl.BlockSpec(w_lo.shape, lambda b, k, j: (0, 0))]
    args = [a4, a4, jnp.asarray(plan.tw), w_hi, w_lo]
    if kx is not None:
        kx4 = kx.reshape(2 * HY_ORDER, 2 * k1p, n2, C)
        kblk = lambda d, off: pl.BlockSpec((None, None, n2, ct), lambda b, k, j: (2 * order + d, off + k, 0, j))
        in_specs += [kblk(0, 0), kblk(0, k1p), kblk(1, 0), kblk(1, k1p)]
        args += [kx4, kx4, kx4, kx4]
    out_r, out_i = pl.pallas_call(
        functools.partial(_stage_b_kernel, n2=n2, ct=ct, k1n=k1n, with_filter=kx is not None),
        grid=(B, k1p, nj),
        in_specs=in_specs,
        out_specs=[pl.BlockSpec((None, None, n2, ct), lambda b, k, j: (b, k, 0, j))] * 2,
        out_shape=[jax.ShapeDtypeStruct((B, k1p, n2, C), F32)] * 2,
        compiler_params=_cparams("parallel", "parallel", "parallel"),
        name="hy_stage_b",
    )(*args)
    return out_r.reshape(B, k1p, n2 * C), out_i.reshape(B, k1p, n2 * C)


def _stage_c_kernel(*refs, k1p, dense):
    if dense:
        a_ref, kf_ref, kb_ref, ghi_ref, glo_ref, z_ref, gate_ref, bias_ref, o_ref = refs
        ar, ai = a_ref[:k1p, :], a_ref[k1p:, :]
        kr = kf_ref[:k1p, :] + kb_ref[:k1p, :]
        ki = kf_ref[k1p:, :] - kb_ref[k1p:, :]
        y = jnp.concatenate([ar * kr - ai * ki, ar * ki + ai * kr], axis=0)
    else:
        yr_ref, yi_ref, ghi_ref, glo_ref, z_ref, gate_ref, bias_ref, o_ref = refs
        y = jnp.concatenate([yr_ref[...], yi_ref[...]], axis=0)
    conv = _dot3(ghi_ref[...], glo_ref[...], y)
    z = z_ref[...]
    o_ref[...] = (gate_ref[...] * (conv + bias_ref[...] * z)).astype(o_ref.dtype)


def hy_stage_c(y, plan, z, gate, bias_n, out_dtype, kx=None, order=0):
    B, L, C = z.shape
    cols = plan.N2 * C
    k1p = plan.K1P
    tc = min(cols, 8192)
    ghi, glo = _const_split(plan.g)
    zv = z.reshape(B, plan.H1, cols)
    gv = gate.reshape(B, plan.H1, cols)
    bias_t = jnp.tile(bias_n, plan.N2).reshape(1, cols)
    dense = plan.N2 == 1
    tok = pl.BlockSpec((None, plan.H1, tc), lambda b, j: (b, 0, j))
    gspec = pl.BlockSpec(ghi.shape, lambda b, j: (0, 0))
    if dense:
        kspec = lambda d: pl.BlockSpec((None, 2 * k1p, tc), lambda b, j: (2 * order + d, 0, j))
        in_specs = [pl.BlockSpec((None, 2 * k1p, tc), lambda b, j: (b, 0, j)), kspec(0), kspec(1)]
        args = [y, kx, kx]
    else:
        ys = pl.BlockSpec((None, k1p, tc), lambda b, j: (b, 0, j))
        in_specs = [ys, ys]
        args = list(y)
    in_specs += [gspec, gspec, tok, tok, pl.BlockSpec((1, tc), lambda b, j: (0, j))]
    args += [ghi, glo, zv, gv, bias_t]
    out = pl.pallas_call(
        functools.partial(_stage_c_kernel, k1p=k1p, dense=dense),
        grid=(B, cols // tc),
        in_specs=in_specs,
        out_specs=tok,
        out_shape=jax.ShapeDtypeStruct((B, plan.H1, cols), out_dtype),
        compiler_params=_cparams("parallel", "parallel"),
        name="hy_stage_c",
    )(*args)
    return out.reshape(B, L, C)


def hyena_mixer(proj3, col0, conv_w, conv_b, w1, b1, w2, b2, w3, freq, decay, bias):
    B, L, _ = proj3.shape
    width = HY_WIDTH
    plan = HyPlan(L, 1 if L <= HY_DENSE_MAX_L else HY_LONG_N2)
    u = hy_conv3(proj3, col0, conv_w, conv_b, width)
    filt = hy_filters(L, w1, b1, w2, b2, w3, freq, decay, width).reshape(2 * HY_ORDER, L, width)
    kx = hy_stage_a(filt, plan)
    dense = plan.N2 == 1
    if not dense:
        kr, ki = hy_stage_b(kx, plan, width)
        kx = jnp.concatenate([kr, ki], axis=1)
    z = u[0]
    for n in range(HY_ORDER):
        a = hy_stage_a(z, plan)
        od = BF16 if n == HY_ORDER - 1 else F32
        if dense:
            z = hy_stage_c(a, plan, z, u[1 + n], bias[n], od, kx=kx, order=n)
        else:
            y = hy_stage_b(a, plan, width, kx=kx, order=n)
            z = hy_stage_c(y, plan, z, u[1 + n], bias[n], od)
    return z


def _time_major(x3, n_cols, pad_to):
    B = x3.shape[0]
    xt = jnp.transpose(x3[..., :n_cols], (1, 0, 2))
    nb = -(-B // pad_to) * pad_to
    if nb != B:
        xt = jnp.pad(xt, ((0, 0), (0, nb - B), (0, 0)))
    return xt


def _pad_chains(x, axis):
    n = x.shape[axis]
    nb = -(-n // CHAINS) * CHAINS
    if nb == n:
        return x
    pad = [(0, 0)] * x.ndim
    pad[axis] = (0, nb - n)
    return jnp.pad(x, pad)


def _batch_major(y_rows, L, B):
    return jnp.transpose(y_rows.reshape(L, B, -1), (1, 0, 2)).reshape(B * L, -1)


def kernel(x_prompt, x_sample, cache_na_k, cache_na_v, state_s5, state_rglru, c, c_ctx, norm_g, mod_w, mod_b, ffn_wg, ffn_wu, ffn_wd, even_w_in, even_w_out, s5_lam_re, s5_lam_im, s5_log_dt, s5_b_re, s5_b_im, s5_c_re, s5_c_im, s5_d, s5_glu_w, s5_glu_b, hy_conv_w, hy_conv_b, hy_w1, hy_b1, hy_w2, hy_b2, hy_w3, hy_freq, hy_decay, hy_bias, odd_w_in, odd_w_out, rg_conv_w, rg_conv_b, rg_wa, rg_ba, rg_wi, rg_bi, rg_lam, na_qn, na_kn, na_rpb):
    bp, lp, _ = x_prompt.shape
    bs, ls, _ = x_sample.shape
    n_p = bp * lp
    n_s = bs * ls
    groups = ((0, n_p, bp, lp), (n_p, n_s, bs, ls))

    cond = jnp.concatenate([c_ctx[None], c, jnp.zeros((COND_ROWS - 1 - bs, D_MODEL), F32)], axis=0)
    m_all = adaln_all(cond, mod_w, mod_b)

    wg = ffn_wg.astype(BF16)
    wu = ffn_wu.astype(BF16)
    wd = ffn_wd.astype(BF16)
    w_in_e = even_w_in.astype(BF16)
    w_out_e = even_w_out.astype(BF16)
    w_in_o = odd_w_in.astype(BF16)
    w_out_o = odd_w_out.astype(BF16)

    x = jnp.concatenate([x_prompt.reshape(n_p, D_MODEL), x_sample.reshape(n_s, D_MODEL)], axis=0)
    s5_new, rg_new, k_new, v_new = [], [], [], []
    for li in range(DEPTH):
        x, h = ffn_block(x, m_all, norm_g, wg, wu, wd, li, 0, n_p, ls)
        if li % 2 == 0:
            ei = li // 2
            proj = in_proj(h, w_in_e, ei)
            prm = s5_params(s5_lam_re[ei], s5_lam_im[ei], s5_log_dt[ei], s5_b_re[ei], s5_b_im[ei],
                            s5_c_re[ei], s5_c_im[ei])
            ya, yb = [], []
            for gi, (t0, nt, B, L) in enumerate(groups):
                p3 = proj[t0:t0 + nt].reshape(B, L, -1)
                u_tm = _time_major(p3, S5_WIDTH, CHAINS)
                if gi == 0:
                    h0 = jnp.zeros((2, u_tm.shape[1], 2 * S5_STATE), F32)
                else:
                    h0 = jnp.transpose(state_s5[:, ei], (1, 0, 4, 2, 3)).reshape(2, B, 2 * S5_STATE)
                    h0 = _pad_chains(h0, 1)
                y, hfin = s5_scan(u_tm, *prm, h0)
                out = s5_glu(u_tm[:, :B].reshape(L * B, S5_WIDTH), y[:, :, :B].reshape(2, L * B, S5_WIDTH),
                             s5_d[ei].reshape(-1), s5_glu_w[ei], s5_glu_b[ei])
                ya.append(_batch_major(out, L, B))
                if gi == 0:
                    fin = hfin[:, :B].reshape(2, B, 2, S5_GROUPS, S5_P)
                    s5_new.append(jnp.transpose(fin, (1, 0, 3, 4, 2)))
                yb.append(hyena_mixer(p3, 1, hy_conv_w[ei], hy_conv_b[ei], hy_w1[ei], hy_b1[ei], hy_w2[ei], hy_b2[ei],
                                      hy_w3[ei], hy_freq[ei], hy_decay[ei], hy_bias[ei]).reshape(nt, HY_WIDTH))
            x = out_proj_residual(jnp.concatenate(ya, axis=0), jnp.concatenate(yb, axis=0), w_out_e, ei, x, m_all,
                                  li, n_p, ls)
        else:
            oi = li // 2
            proj = in_proj(h, w_in_o, oi)
            gprm = rglru_params(rg_wa[oi], rg_ba[oi], rg_wi[oi], rg_bi[oi])
            q, k, v, k_f32 = qkv_prep(proj, 2, na_qn[oi], na_kn[oi])
            yc, yd = [], []
            for gi, (t0, nt, B, L) in enumerate(groups):
                p3 = proj[t0:t0 + nt].reshape(B, L, -1)
                x_tm = _time_major(p3, 2 * RG_WIDTH, CHAINS)
                if gi == 0:
                    h0 = jnp.zeros((2, x_tm.shape[1], RG_WIDTH), F32)
                else:
                    h0 = _pad_chains(jnp.transpose(state_rglru[:, oi], (1, 0, 2)), 1)
                hs, hfin = rglru_scan(x_tm, 0, rg_conv_w[oi], rg_conv_b[oi], *gprm, rg_lam[oi], h0)
                out = rg_combine(hs[:, :, :B].reshape(2, L * B, RG_WIDTH), x_tm[:, :B].reshape(L * B, 2 * RG_WIDTH), 1)
                yc.append(_batch_major(out, L, B))
                if gi == 0:
                    rg_new.append(jnp.transpose(hfin[:, :B], (1, 0, 2)))
                    yd.append(ctx_attention(q, k, v, B, L))
                    k_new.append(k_f32[:nt].reshape(B, L, NA_HEADS, NA_HD))
                    v_new.append(p3[..., 4 * NA_WIDTH:].reshape(B, L, NA_HEADS, NA_HD))
                else:
                    past = cache_na_k.shape[2]
                    kc = cache_na_k[:, oi].reshape(B, past, NA_WIDTH).astype(BF16)
                    vc = cache_na_v[:, oi].reshape(B, past, NA_WIDTH).astype(BF16)
                    yd.append(na_attention(q, k, v, kc, vc, na_bias_table(na_rpb[oi]), t0, B, L))
            x = out_proj_residual(jnp.concatenate(yc, axis=0), jnp.concatenate(yd, axis=0), w_out_o, oi, x, m_all,
                                  li, n_p, ls)
        x, _ = ffn_block(x, m_all, norm_g, wg, wu, wd, li, 1, n_p, ls)

    y_prompt = x[:n_p].reshape(bp, lp, D_MODEL)
    y_sample = x[n_p:].reshape(bs, ls, D_MODEL)
    new_na_k = jnp.stack(k_new, axis=1)
    new_na_v = jnp.stack(v_new, axis=1)
    new_s5 = jnp.stack(s5_new, axis=1)
    new_rglru = jnp.stack(rg_new, axis=1)
    return (y_prompt, y_sample, new_na_k, new_na_v, new_s5, new_rglru)
```

```python
import functools
import math

import numpy as np
import jax
import jax.numpy as jnp
from jax import lax
from jax.experimental import pallas as pl
from jax.experimental.pallas import tpu as pltpu

F32 = jnp.float32
BF16 = jnp.bfloat16

D_MODEL = 2048
DEPTH = 4
N_MOD = 9
D_FF = 5632
EPS = 1e-6
MIX_HALF = D_MODEL // 2
GRID_W = 64
S5_WIDTH = MIX_HALF
S5_H = 16
S5_GROUPS = S5_WIDTH // S5_H
S5_P = 64
S5_STATE = S5_GROUPS * S5_P
S5_LAM_RE_MAX = -1e-4
HY_WIDTH = MIX_HALF
HY_ORDER = 2
HY_SHORT = 3
HY_BANDS = 16
HY_EMB = 1 + 2 * HY_BANDS
HY_EMB_PAD = 40
HY_FFN = 64
RG_WIDTH = MIX_HALF
RG_BLOCKS = 16
RG_BW = RG_WIDTH // RG_BLOCKS
RG_CONV = 4
RG_LEFT = RG_CONV // 2
RG_C = 8.0
NA_HD = 64
NA_HEADS = MIX_HALF // NA_HD
NA_WIDTH = NA_HEADS * NA_HD
NA_WIN_R = 8
NA_WIN_C = 16

VMEM_LIMIT_BYTES = 56 * 1024 * 1024
LANES = 128
CHAINS = 8
COND_ROWS = 8
TOKEN_TILE = 512
FF_TILE = 512
SCAN_T = 64
S5_COLS = 512
S5_BBLK = 256
S5_CBLK = 128
RG_GBLK = 256
HEAD_CHUNK = 256
NEG_BIG = -1e30
HY_BLOCK = 512


def _cparams(*sem):
    return pltpu.CompilerParams(dimension_semantics=sem, vmem_limit_bytes=VMEM_LIMIT_BYTES)


def _split_bf16(x):
    hi = x.astype(BF16)
    lo = (x - hi.astype(F32)).astype(BF16)
    return hi, lo


def _dot3(a_hi, a_lo, x):
    x_hi, x_lo = _split_bf16(x)
    return (jnp.dot(a_hi, x_hi, preferred_element_type=F32) + jnp.dot(a_lo, x_hi, preferred_element_type=F32)
            + jnp.dot(a_hi, x_lo, preferred_element_type=F32))


def _dot3_r(x, b_hi, b_lo):
    x_hi, x_lo = _split_bf16(x)
    return (jnp.dot(x_hi, b_hi, preferred_element_type=F32) + jnp.dot(x_hi, b_lo, preferred_element_type=F32)
            + jnp.dot(x_lo, b_hi, preferred_element_type=F32))


def _const_split(m):
    return _split_bf16(jnp.asarray(np.asarray(m, np.float32)))


def _gelu_tanh(x):
    return 0.5 * x * (1.0 + jnp.tanh(math.sqrt(2.0 / math.pi) * (x + 0.044715 * (x * x * x))))


def _adaln_kernel(cond_ref, w_ref, b_ref, o_ref):
    c = cond_ref[...]
    s = (c * jax.nn.sigmoid(c)).astype(BF16)
    o_ref[...] = jnp.dot(s, w_ref[...].astype(BF16), preferred_element_type=F32) + b_ref[...]


def adaln_all(cond, mod_w, mod_b):
    tn = 1024
    n_out = N_MOD * D_MODEL
    out = pl.pallas_call(
        _adaln_kernel,
        grid=(DEPTH, n_out // tn),
        in_specs=[
            pl.BlockSpec((COND_ROWS, D_MODEL), lambda l, j: (0, 0)),
            pl.BlockSpec((None, D_MODEL, tn), lambda l, j: (l, 0, j)),
            pl.BlockSpec((None, 1, tn), lambda l, j: (l, 0, j)),
        ],
        out_specs=pl.BlockSpec((None, COND_ROWS, tn), lambda l, j: (l, 0, j)),
        out_shape=jax.ShapeDtypeStruct((DEPTH, COND_ROWS, n_out), F32),
        compiler_params=_cparams("parallel", "parallel"),
        name="adaln",
    )(cond, mod_w, mod_b.reshape(DEPTH, 1, n_out))
    return out.reshape(DEPTH, COND_ROWS, N_MOD, D_MODEL)


def _rms_modulate(x, g, shift, scale):
    ms = jnp.mean(x * x, axis=-1, keepdims=True)
    return x * lax.rsqrt(ms + EPS) * g * (1.0 + scale) + shift


def _ffn_kernel(x_ref, m_ref, g_ref, wg_ref, wu_ref, wd_ref, o_ref, *rest,
                n_ff, in_rows, gate_row, g_row, next_rows, next_g_row):
    hn_ref = rest[0] if next_rows is not None else None
    h_s, acc_s = rest[-2:]
    k = pl.program_id(1)

    @pl.when(k == 0)
    def _():
        h = _rms_modulate(x_ref[...], g_ref[g_row:g_row + 1, :],
                          m_ref[in_rows[0]:in_rows[0] + 1, :], m_ref[in_rows[1]:in_rows[1] + 1, :])
        h_s[...] = h.astype(BF16)
        acc_s[...] = jnp.zeros_like(acc_s)

    h = h_s[...]
    gt = jnp.dot(h, wg_ref[...], preferred_element_type=F32)
    ut = jnp.dot(h, wu_ref[...], preferred_element_type=F32)
    a = (gt * jax.nn.sigmoid(gt) * ut).astype(BF16)
    acc_s[...] += jnp.dot(a, wd_ref[...], preferred_element_type=F32)

    @pl.when(k == n_ff - 1)
    def _():
        xn = x_ref[...] + 0.5 * m_ref[gate_row:gate_row + 1, :] * acc_s[...]
        o_ref[...] = xn
        if hn_ref is not None:
            hn = _rms_modulate(xn, g_ref[next_g_row:next_g_row + 1, :],
                               m_ref[next_rows[0]:next_rows[0] + 1, :], m_ref[next_rows[1]:next_rows[1] + 1, :])
            hn_ref[...] = hn.astype(BF16)


def _cond_row(i, tm, n_prompt_tok, sample_len):
    return jnp.where(i * tm < n_prompt_tok, 0, 1 + (i * tm - n_prompt_tok) // sample_len)


def ffn_weights(wg, wu, wd):
    n_ff = D_FF // FF_TILE
    tile_cols = lambda w: jnp.transpose(w.astype(BF16).reshape(DEPTH, 2, D_MODEL, n_ff, FF_TILE), (0, 1, 3, 2, 4))
    return tile_cols(wg), tile_cols(wu), wd.astype(BF16)


def ffn_block(x, m_all, norm_g, wg, wu, wd, li, half, n_prompt_tok, sample_len):
    n_tok = x.shape[0]
    tm, tf = TOKEN_TILE, FF_TILE
    n_ff = D_FF // tf
    if half == 0:
        cfg = dict(in_rows=(0, 1), gate_row=2, g_row=0, next_rows=(3, 4), next_g_row=1)
    else:
        cfg = dict(in_rows=(6, 7), gate_row=8, g_row=2, next_rows=None, next_g_row=None)
    row = functools.partial(_cond_row, tm=tm, n_prompt_tok=n_prompt_tok, sample_len=sample_len)
    tok = pl.BlockSpec((tm, D_MODEL), lambda i, k: (i, 0))
    out_specs = [tok]
    out_shape = [jax.ShapeDtypeStruct((n_tok, D_MODEL), F32)]
    if half == 0:
        out_specs.append(tok)
        out_shape.append(jax.ShapeDtypeStruct((n_tok, D_MODEL), BF16))
    return pl.pallas_call(
        functools.partial(_ffn_kernel, n_ff=n_ff, **cfg),
        grid=(n_tok // tm, n_ff),
        in_specs=[
            tok,
            pl.BlockSpec((None, None, N_MOD, D_MODEL), lambda i, k: (li, row(i), 0, 0)),
            pl.BlockSpec((None, 3, D_MODEL), lambda i, k: (li, 0, 0)),
            pl.BlockSpec((None, None, None, D_MODEL, tf), lambda i, k: (li, half, k, 0, 0)),
            pl.BlockSpec((None, None, None, D_MODEL, tf), lambda i, k: (li, half, k, 0, 0)),
            pl.BlockSpec((None, None, tf, D_MODEL), lambda i, k: (li, half, k, 0)),
        ],
        out_specs=out_specs,
        out_shape=out_shape,
        scratch_shapes=[pltpu.VMEM((tm, D_MODEL), BF16), pltpu.VMEM((tm, D_MODEL), F32)],
        compiler_params=_cparams("parallel", "arbitrary"),
        name=f"ffn_l{li}_h{half}",
    )(x, m_all, norm_g, wg, wu, wd)


def _proj_kernel(a_ref, w_ref, o_ref):
    o_ref[...] = jnp.dot(a_ref[...], w_ref[...], preferred_element_type=F32).astype(o_ref.dtype)


def in_proj(h, w, idx):
    n_tok = h.shape[0]
    n_out = w.shape[-1]
    tm, tn = TOKEN_TILE, n_out // 2
    return pl.pallas_call(
        _proj_kernel,
        grid=(n_out // tn, n_tok // tm),
        in_specs=[pl.BlockSpec((tm, D_MODEL), lambda j, i: (i, 0)),
                  pl.BlockSpec((None, D_MODEL, tn), lambda j, i: (idx, 0, j))],
        out_specs=pl.BlockSpec((tm, tn), lambda j, i: (i, j)),
        out_shape=jax.ShapeDtypeStruct((n_tok, n_out), F32),
        compiler_params=_cparams("parallel", "parallel"),
        name="in_proj",
    )(h, w)


def _out_proj_kernel(ya_ref, yb_ref, wa_ref, wb_ref, x_ref, m_ref, o_ref):
    y = (jnp.dot(ya_ref[...], wa_ref[...], preferred_element_type=F32)
         + jnp.dot(yb_ref[...], wb_ref[...], preferred_element_type=F32))
    o_ref[...] = x_ref[...] + m_ref[5:6, :] * y


def out_proj_residual(ya, yb, w, idx, x, m_all, li, n_prompt_tok, sample_len):
    n_tok = x.shape[0]
    tm = TOKEN_TILE
    row = functools.partial(_cond_row, tm=tm, n_prompt_tok=n_prompt_tok, sample_len=sample_len)
    return pl.pallas_call(
        _out_proj_kernel,
        grid=(n_tok // tm,),
        in_specs=[pl.BlockSpec((tm, MIX_HALF), lambda i: (i, 0)),
                  pl.BlockSpec((tm, MIX_HALF), lambda i: (i, 0)),
                  pl.BlockSpec((None, MIX_HALF, D_MODEL), lambda i: (idx, 0, 0)),
                  pl.BlockSpec((None, MIX_HALF, D_MODEL), lambda i: (idx, 1, 0)),
                  pl.BlockSpec((tm, D_MODEL), lambda i: (i, 0)),
                  pl.BlockSpec((None, None, N_MOD, D_MODEL), lambda i: (li, row(i), 0, 0))],
        out_specs=pl.BlockSpec((tm, D_MODEL), lambda i: (i, 0)),
        out_shape=jax.ShapeDtypeStruct((n_tok, D_MODEL), F32),
        compiler_params=_cparams("parallel"),
        name="out_proj",
    )(ya, yb, w, w, x, m_all)


def _s5_scan_kernel(u_ref, bw_ref, a_ref, cw_ref, h0_ref, *rest, n_t, t_tile, with_y):
    if with_y:
        y_ref, hfin_ref, br_s, bi_s, hr_s, hi_s = rest
    else:
        hfin_ref, br_s, bi_s, hr_s, hi_s = rest
    d = pl.program_id(0)
    jt = pl.program_id(2)
    rows = t_tile * CHAINS

    @pl.when(jt == 0)
    def _():
        hr_s[...] = h0_ref[:, :S5_STATE]
        hi_s[...] = h0_ref[:, S5_STATE:]

    u = u_ref[...].reshape(rows, S5_WIDTH).astype(BF16)
    n_bblk = S5_WIDTH // S5_BBLK
    ncol = S5_STATE // n_bblk
    for blk in range(n_bblk):
        bu = jnp.dot(u[:, blk * S5_BBLK:(blk + 1) * S5_BBLK], bw_ref[blk], preferred_element_type=F32)
        br_s[:, blk * ncol:(blk + 1) * ncol] = bu[:, :ncol]
        bi_s[:, blk * ncol:(blk + 1) * ncol] = bu[:, ncol:]

    for cc in range(S5_STATE // S5_COLS):
        cs = slice(cc * S5_COLS, (cc + 1) * S5_COLS)
        ar = a_ref[0, :, cs]
        ai = a_ref[1, :, cs]

        def step(i, carry):
            hr, hi = carry
            t = jnp.where(d == 0, i, t_tile - 1 - i)
            r0 = pl.multiple_of(t * CHAINS, CHAINS)
            nhr = ar * hr - ai * hi + br_s[pl.ds(r0, CHAINS), cs]
            nhi = ar * hi + ai * hr + bi_s[pl.ds(r0, CHAINS), cs]
            if with_y:
                br_s[pl.ds(r0, CHAINS), cs] = nhr
                bi_s[pl.ds(r0, CHAINS), cs] = nhi
            return nhr, nhi

        hr, hi = lax.fori_loop(0, t_tile, step, (hr_s[:, cs], hi_s[:, cs]), unroll=4)
        hr_s[:, cs] = hr
        hi_s[:, cs] = hi

    if with_y:
        n_cblk = S5_WIDTH // S5_CBLK
        kc = S5_STATE // n_cblk
        for i in range(n_cblk):
            yi = jnp.dot(br_s[:, i * kc:(i + 1) * kc].astype(BF16), cw_ref[0, i], preferred_element_type=F32)
            yi = yi + jnp.dot(bi_s[:, i * kc:(i + 1) * kc].astype(BF16), cw_ref[1, i], preferred_element_type=F32)
            y_ref[:, :, i * S5_CBLK:(i + 1) * S5_CBLK] = yi.reshape(t_tile, CHAINS, S5_CBLK)

    @pl.when(jt == n_t - 1)
    def _():
        hfin_ref[:, :S5_STATE] = hr_s[...]
        hfin_ref[:, S5_STATE:] = hi_s[...]


def s5_scan(u_tm, bw, a_b, cw, h0, with_y=True):
    L, nc, _ = u_tm.shape
    t_tile = min(SCAN_T, L)
    n_t = L // t_tile
    tix = lambda d, j: jnp.where(d == 0, j, n_t - 1 - j)
    fin_spec = pl.BlockSpec((None, CHAINS, 2 * S5_STATE), lambda d, c, j: (d, c, 0))
    fin_shape = jax.ShapeDtypeStruct((2, nc, 2 * S5_STATE), F32)
    out_specs, out_shape = [fin_spec], [fin_shape]
    if with_y:
        out_specs = [pl.BlockSpec((None, t_tile, CHAINS, S5_WIDTH), lambda d, c, j: (d, tix(d, j), c, 0))] + out_specs
        out_shape = [jax.ShapeDtypeStruct((2, L, nc, S5_WIDTH), F32)] + out_shape
    return pl.pallas_call(
        functools.partial(_s5_scan_kernel, n_t=n_t, t_tile=t_tile, with_y=with_y),
        grid=(2, nc // CHAINS, n_t),
        in_specs=[
            pl.BlockSpec((t_tile, CHAINS, S5_WIDTH), lambda d, c, j: (tix(d, j), c, 0)),
            pl.BlockSpec((None,) + bw.shape[1:], lambda d, c, j: (d, 0, 0, 0)),
            pl.BlockSpec((None,) + a_b.shape[1:], lambda d, c, j: (d, 0, 0, 0)),
            pl.BlockSpec((None,) + cw.shape[1:], lambda d, c, j: (d, 0, 0, 0, 0)),
            fin_spec,
        ],
        out_specs=out_specs,
        out_shape=out_shape,
        scratch_shapes=[pltpu.VMEM((t_tile * CHAINS, S5_STATE), F32), pltpu.VMEM((t_tile * CHAINS, S5_STATE), F32),
                        pltpu.VMEM((CHAINS, S5_STATE), F32), pltpu.VMEM((CHAINS, S5_STATE), F32)],
        compiler_params=_cparams("parallel", "parallel", "arbitrary"),
        name="s5_scan" if with_y else "s5_state",
    )(u_tm, bw, a_b, cw, h0)


def _seg_init_kernel(hl_ref, h0_ref, ar_ref, ai_ref, o_ref, *, n_seg, width, cplx):
    cr = h0_ref[:, :width]
    ci = h0_ref[:, width:] if cplx else None
    for k in range(n_seg):
        o_ref[k, :, :width] = cr
        ar = ar_ref[k]
        if cplx:
            o_ref[k, :, width:] = ci
            ai = ai_ref[k]
            cr, ci = (ar * cr - ai * ci + hl_ref[k, :, :width], ar * ci + ai * cr + hl_ref[k, :, width:])
        else:
            cr = ar * cr + hl_ref[k]


def seg_init(hl, h0, ar, ai=None):
    _, S, B, W2 = hl.shape
    cplx = ai is not None
    width = W2 // 2 if cplx else W2
    if ai is None:
        ai = ar
    aspec = pl.BlockSpec((None,) + ar.shape[1:], lambda d: (d, 0, 0, 0))
    return pl.pallas_call(
        functools.partial(_seg_init_kernel, n_seg=S, width=width, cplx=cplx),
        grid=(2,),
        in_specs=[pl.BlockSpec((None, S, B, W2), lambda d: (d, 0, 0, 0)),
                  pl.BlockSpec((None, B, W2), lambda d: (d, 0, 0)), aspec, aspec],
        out_specs=pl.BlockSpec((None, S, B, W2), lambda d: (d, 0, 0, 0)),
        out_shape=jax.ShapeDtypeStruct((2, S, B, W2), F32),
        compiler_params=_cparams("parallel"),
        name="seg_init",
    )(hl, h0, ar, ai)


def _to_proc_order(x, B, S):
    x = jnp.transpose(x.reshape(2, B, S, -1), (0, 2, 1, 3))
    return jnp.stack([x[0], x[1, ::-1]], axis=0)


def _from_proc_order(x, B, S):
    x = jnp.stack([x[0], x[1, ::-1]], axis=0)
    return jnp.transpose(x, (0, 2, 1, 3)).reshape(2, B * S, -1)


def _cpow(ar, ai, n):
    rr, ri = None, None
    br, bi = ar, ai
    while n:
        if n & 1:
            rr, ri = (br, bi) if rr is None else (rr * br - ri * bi, rr * bi + ri * br)
        n >>= 1
        if n:
            br, bi = br * br - bi * bi, 2.0 * br * bi
    return rr, ri


def s5_segmented(u8, prm, h0, B, S):
    bw, a_b, cw = prm
    Ls = u8.shape[0]
    zeros = jnp.zeros((2, B * S, 2 * S5_STATE), F32)
    (hloc,) = s5_scan(u8, bw, a_b, cw, zeros, with_y=False)
    pr, pi = _cpow(a_b[:, 0, 0], a_b[:, 1, 0], Ls)
    rep = lambda p: jnp.broadcast_to(p[:, None, None, :], (2, S, 1, S5_STATE))
    init = seg_init(_to_proc_order(hloc, B, S), h0, rep(pr), rep(pi))
    y, _ = s5_scan(u8, bw, a_b, cw, _from_proc_order(init, B, S), with_y=True)
    return y


def s5_params(lam_re, lam_im, log_dt, b_re, b_im, c_re, c_im):
    lr = jnp.minimum(lam_re, S5_LAM_RE_MAX)
    li = lam_im
    dt = jnp.exp(log_dt)[..., None]
    mag = jnp.exp(lr * dt)
    abr, abi = mag * jnp.cos(li * dt), mag * jnp.sin(li * dt)
    den = lr * lr + li * li
    cr = ((abr - 1.0) * lr + abi * li) / den
    ci = (abi * lr - (abr - 1.0) * li) / den
    bbr = cr[..., None] * b_re - ci[..., None] * b_im
    bbi = cr[..., None] * b_im + ci[..., None] * b_re
    n_bblk = S5_WIDTH // S5_BBLK
    gb = S5_GROUPS // n_bblk
    bb = jnp.stack([bbr, bbi], axis=1).reshape(2, 2, n_bblk, gb, S5_P, S5_H)
    bw = jnp.einsum('drbgph,gk->dbghrkp', bb, jnp.eye(gb, dtype=F32)).reshape(2, n_bblk, gb * S5_H, 2 * gb * S5_P)
    n_cblk = S5_WIDTH // S5_CBLK
    gc = S5_GROUPS // n_cblk
    cc = jnp.stack([c_re, -c_im], axis=1).reshape(2, 2, n_cblk, gc, S5_H, S5_P)
    cw = jnp.einsum('drbghp,gk->drbgpkh', cc, jnp.eye(gc, dtype=F32)).reshape(2, 2, n_cblk, gc * S5_P, gc * S5_H)
    a_b = jnp.stack([abr.reshape(2, S5_STATE), abi.reshape(2, S5_STATE)], axis=1)
    a_b = jnp.broadcast_to(a_b[:, :, None, :], (2, 2, CHAINS, S5_STATE))
    return bw.astype(BF16), a_b, cw.astype(BF16)


def _s5_glu_kernel(u_ref, y_ref, dsk_ref, w_ref, b_ref, o_ref):
    y = dsk_ref[...] * u_ref[...] + y_ref[0] + y_ref[1]
    z = _gelu_tanh(y)
    gate = jnp.dot(z.astype(BF16), w_ref[...], preferred_element_type=F32) + b_ref[...]
    o_ref[...] = (z * jax.nn.sigmoid(gate)).astype(o_ref.dtype)


def s5_glu(u2d, y2d, d_skip, glu_w, glu_b):
    R = u2d.shape[0]
    tm = TOKEN_TILE
    return pl.pallas_call(
        _s5_glu_kernel,
        grid=(R // tm,),
        in_specs=[pl.BlockSpec((tm, S5_WIDTH), lambda i: (i, 0)),
                  pl.BlockSpec((2, tm, S5_WIDTH), lambda i: (0, i, 0)),
                  pl.BlockSpec((1, S5_WIDTH), lambda i: (0, 0)),
                  pl.BlockSpec((S5_WIDTH, S5_WIDTH), lambda i: (0, 0)),
                  pl.BlockSpec((1, S5_WIDTH), lambda i: (0, 0))],
        out_specs=pl.BlockSpec((tm, S5_WIDTH), lambda i: (i, 0)),
        out_shape=jax.ShapeDtypeStruct((R, S5_WIDTH), BF16),
        compiler_params=_cparams("parallel"),
        name="s5_glu",
    )(u2d, y2d, d_skip.reshape(1, S5_WIDTH), glu_w.astype(BF16), glu_b.reshape(1, S5_WIDTH))


def _rglru_kernel(xp_ref, xc_ref, xn_ref, cw_ref, cb_ref, wa_ref, ba_ref, wi_ref, bi_ref, lam_ref, h0_ref, *rest,
                  n_t, t_tile, seg, n_seg):
    if seg:
        h_ref, p_ref, hfin_ref, pfin_ref, a_s, b_s, hc_s, pc_s = rest
    else:
        h_ref, hfin_ref, a_s, b_s, hc_s = rest
    d = pl.program_id(0)
    jt = pl.program_id(2)
    tj = jnp.where(d == 0, jt, n_t - 1 - jt)
    rows = t_tile * CHAINS

    @pl.when(jt == 0)
    def _():
        hc_s[...] = h0_ref[...]
        if seg:
            pc_s[...] = jnp.ones_like(pc_s)

    prev_in = xp_ref[...]
    next_in = xn_ref[...]
    if seg:
        cidx = lax.broadcasted_iota(jnp.int32, (1, CHAINS, 1), 1) % n_seg
        first_t = tj == 0
        last_t = tj == n_t - 1
        prev_nb = pltpu.roll(prev_in, 1, axis=1)
        next_nb = pltpu.roll(next_in, CHAINS - 1, axis=1)
        prev = jnp.where(first_t, jnp.where(cidx == 0, 0.0, prev_nb), prev_in)
        nxt = jnp.where(last_t, jnp.where(cidx == n_seg - 1, 0.0, next_nb), next_in)
    else:
        prev = prev_in * (tj > 0).astype(F32)
        nxt = next_in * (tj < n_t - 1).astype(F32)
    xcat = jnp.concatenate([prev, xc_ref[...], nxt], axis=0)
    xc = cb_ref[...] + cw_ref[0:1, :] * xcat[0:t_tile]
    for j in range(1, RG_CONV):
        xc = xc + cw_ref[j:j + 1, :] * xcat[j:j + t_tile]
    x2 = xc.reshape(rows, RG_WIDTH)
    xb = x2.astype(BF16)

    ra, ia = [], []
    for j in range(RG_WIDTH // RG_GBLK):
        xs = xb[:, j * RG_GBLK:(j + 1) * RG_GBLK]
        ra.append(jnp.dot(xs, wa_ref[j], preferred_element_type=F32))
        ia.append(jnp.dot(xs, wi_ref[j], preferred_element_type=F32))
    r = jax.nn.sigmoid(jnp.concatenate(ra, axis=1) + ba_ref[...])
    ig = jax.nn.sigmoid(jnp.concatenate(ia, axis=1) + bi_ref[...])
    nl = -lam_ref[...]
    sp = jnp.maximum(nl, 0.0) + jnp.log1p(jnp.exp(-jnp.abs(nl)))
    log_a = (-RG_C) * r * sp
    a_s[...] = jnp.exp(log_a)
    th = jnp.tanh(log_a)
    one_minus_a2 = (-2.0) * th / (1.0 - th)
    b_s[...] = jnp.sqrt(one_minus_a2) * (ig * x2)

    def step(i, carry):
        t = jnp.where(d == 0, i, t_tile - 1 - i)
        r0 = pl.multiple_of(t * CHAINS, CHAINS)
        a = a_s[pl.ds(r0, CHAINS), :]
        if seg:
            h, p = carry
            h = a * h + b_s[pl.ds(r0, CHAINS), :]
            p = a * p
            h_ref[t] = h
            p_ref[t] = p
            return h, p
        h = a * carry + b_s[pl.ds(r0, CHAINS), :]
        h_ref[t] = h
        return h

    if seg:
        h, p = lax.fori_loop(0, t_tile, step, (hc_s[...], pc_s[...]), unroll=8)
        pc_s[...] = p
    else:
        h = lax.fori_loop(0, t_tile, step, hc_s[...], unroll=8)
    hc_s[...] = h

    @pl.when(jt == n_t - 1)
    def _():
        hfin_ref[...] = h
        if seg:
            pfin_ref[...] = p


def rglru_scan(x_tm, col_blk, conv_w, conv_b, wa, ba, wi, bi, lam, h0, n_seg=1):
    L, nc, _ = x_tm.shape
    seg = n_seg > 1
    assert not seg or nc == CHAINS
    t_tile = min(SCAN_T, L)
    n_t = L // t_tile
    tix = lambda d, j: jnp.where(d == 0, j, n_t - 1 - j)
    half = t_tile // RG_LEFT
    if seg:
        prev_ix = lambda d, c, j: ((tix(d, j) * half - 1) % (L // RG_LEFT), c, col_blk)
        next_ix = lambda d, c, j: (((tix(d, j) + 1) * t_tile) % L, c, col_blk)
    else:
        prev_ix = lambda d, c, j: (jnp.maximum(tix(d, j) * half - 1, 0), c, col_blk)
        next_ix = lambda d, c, j: (jnp.minimum((tix(d, j) + 1) * t_tile, L - 1), c, col_blk)
    hspec = pl.BlockSpec((None, t_tile, CHAINS, RG_WIDTH), lambda d, c, j: (d, tix(d, j), c, 0))
    fspec = pl.BlockSpec((None, CHAINS, RG_WIDTH), lambda d, c, j: (d, c, 0))
    hshape = jax.ShapeDtypeStruct((2, L, nc, RG_WIDTH), F32)
    fshape = jax.ShapeDtypeStruct((2, nc, RG_WIDTH), F32)
    scratch = [pltpu.VMEM((t_tile * CHAINS, RG_WIDTH), F32), pltpu.VMEM((t_tile * CHAINS, RG_WIDTH), F32),
               pltpu.VMEM((CHAINS, RG_WIDTH), F32)]
    if seg:
        out_specs, out_shape = [hspec, hspec, fspec, fspec], [hshape, hshape, fshape, fshape]
        scratch.append(pltpu.VMEM((CHAINS, RG_WIDTH), F32))
    else:
        out_specs, out_shape = [hspec, fspec], [hshape, fshape]
    return pl.pallas_call(
        functools.partial(_rglru_kernel, n_t=n_t, t_tile=t_tile, seg=seg, n_seg=n_seg),
        grid=(2, nc // CHAINS, n_t),
        in_specs=[
            pl.BlockSpec((RG_LEFT, CHAINS, RG_WIDTH), prev_ix),
            pl.BlockSpec((t_tile, CHAINS, RG_WIDTH), lambda d, c, j: (tix(d, j), c, col_blk)),
            pl.BlockSpec((1, CHAINS, RG_WIDTH), next_ix),
            pl.BlockSpec((RG_CONV, RG_WIDTH), lambda d, c, j: (0, 0)),
            pl.BlockSpec((1, RG_WIDTH), lambda d, c, j: (0, 0)),
            pl.BlockSpec((None,) + wa.shape[1:], lambda d, c, j: (d, 0, 0, 0)),
            pl.BlockSpec((None, 1, RG_WIDTH), lambda d, c, j: (d, 0, 0)),
            pl.BlockSpec((None,) + wi.shape[1:], lambda d, c, j: (d, 0, 0, 0)),
            pl.BlockSpec((None, 1, RG_WIDTH), lambda d, c, j: (d, 0, 0)),
            pl.BlockSpec((None, 1, RG_WIDTH), lambda d, c, j: (d, 0, 0)),
            fspec,
        ],
        out_specs=out_specs,
        out_shape=out_shape,
        scratch_shapes=scratch,
        compiler_params=_cparams("parallel", "parallel", "arbitrary"),
        name="rglru_seg" if seg else "rglru_scan",
    )(x_tm, x_tm, x_tm, conv_w, conv_b.reshape(1, RG_WIDTH), wa, ba, wi, bi, lam.reshape(2, 1, RG_WIDTH), h0)


def rglru_params(w_a, b_a, w_i, b_i):
    nb = RG_WIDTH // RG_GBLK
    per = RG_GBLK // RG_BW
    eye = jnp.eye(per, dtype=F32)

    def bd(w):
        w = w.reshape(2, nb, per, RG_BW, RG_BW)
        return jnp.einsum('dbnce,nm->dbncme', w, eye).reshape(2, nb, RG_GBLK, RG_GBLK).astype(BF16)

    return bd(w_a), b_a.reshape(2, 1, RG_WIDTH), bd(w_i), b_i.reshape(2, 1, RG_WIDTH)


def _rg_combine_kernel(h_ref, g_ref, *rest, seg):
    if seg:
        p_ref, i_ref, o_ref = rest
        hsum = h_ref[0] + p_ref[0] * i_ref[0] + h_ref[1] + p_ref[1] * i_ref[1]
    else:
        (o_ref,) = rest
        hsum = h_ref[0] + h_ref[1]
    o_ref[...] = (hsum * _gelu_tanh(g_ref[...])).astype(o_ref.dtype)


def rg_combine(h, g_tm, col_blk, p=None, init=None):
    _, L, nc, _ = h.shape
    tt = min(SCAN_T, L)
    seg = p is not None
    hspec = pl.BlockSpec((2, tt, CHAINS, RG_WIDTH), lambda c, j: (0, j, c, 0))
    in_specs = [hspec, pl.BlockSpec((tt, CHAINS, RG_WIDTH), lambda c, j: (j, c, col_blk))]
    args = [h, g_tm]
    if seg:
        in_specs += [hspec, pl.BlockSpec((2, 1, CHAINS, RG_WIDTH), lambda c, j: (0, 0, c, 0))]
        args += [p, init.reshape(2, 1, nc, RG_WIDTH)]
    return pl.pallas_call(
        functools.partial(_rg_combine_kernel, seg=seg),
        grid=(nc // CHAINS, L // tt),
        in_specs=in_specs,
        out_specs=pl.BlockSpec((tt, CHAINS, RG_WIDTH), lambda c, j: (j, c, 0)),
        out_shape=jax.ShapeDtypeStruct((L, nc, RG_WIDTH), BF16),
        compiler_params=_cparams("parallel", "parallel"),
        name="rg_combine",
    )(*args)


def rglru_mixer_tm(x_tm, conv_w, conv_b, gprm, lam, h0, B, S):
    if S == 1:
        h, hfin = rglru_scan(x_tm, 0, conv_w, conv_b, *gprm, lam, h0)
        return rg_combine(h, x_tm, 1), hfin
    zeros = jnp.zeros((2, B * S, RG_WIDTH), F32)
    h, p, hfin, pfin = rglru_scan(x_tm, 0, conv_w, conv_b, *gprm, lam, zeros, n_seg=S)
    init = seg_init(_to_proc_order(hfin, B, S), h0, _to_proc_order(pfin, B, S))
    init = _from_proc_order(init, B, S)
    return rg_combine(h, x_tm, 1, p=p, init=init), hfin


def _head_rms(x, ones_bd, gain):
    sq = x * x
    hi, lo = _split_bf16(sq)
    parts = []
    for j in range(NA_WIDTH // HEAD_CHUNK):
        cs = slice(j * HEAD_CHUNK, (j + 1) * HEAD_CHUNK)
        parts.append(jnp.dot(hi[:, cs], ones_bd, preferred_element_type=F32)
                     + jnp.dot(lo[:, cs], ones_bd, preferred_element_type=F32))
    ms = jnp.concatenate(parts, axis=1) * (1.0 / NA_HD)
    return x * lax.rsqrt(ms + EPS) * gain


def _qkv_prep_kernel(q_ref, k_ref, v_ref, ones_ref, gq_ref, gk_ref, qo_ref, ko_ref, vo_ref, kf_ref):
    ones_bd = ones_ref[...]
    qo_ref[...] = _head_rms(q_ref[...], ones_bd, gq_ref[...]).astype(BF16)
    kn = _head_rms(k_ref[...], ones_bd, gk_ref[...])
    ko_ref[...] = kn.astype(BF16)
    kf_ref[...] = kn
    vo_ref[...] = v_ref[...].astype(BF16)


def qkv_prep(proj, col0, qn_g, kn_g):
    n_tok = proj.shape[0]
    tm = TOKEN_TILE
    ones_bd = jnp.kron(jnp.eye(HEAD_CHUNK // NA_HD, dtype=F32), jnp.ones((NA_HD, NA_HD), F32)).astype(BF16)
    gq = jnp.tile(qn_g, NA_HEADS).reshape(1, NA_WIDTH)
    gk = jnp.tile(kn_g, NA_HEADS).reshape(1, NA_WIDTH)
    blk = lambda c: pl.BlockSpec((tm, NA_WIDTH), lambda i: (i, c))
    full = lambda s: pl.BlockSpec(s, lambda i: (0, 0))
    return pl.pallas_call(
        _qkv_prep_kernel,
        grid=(n_tok // tm,),
        in_specs=[blk(col0), blk(col0 + 1), blk(col0 + 2), full((HEAD_CHUNK, HEAD_CHUNK)),
                  full((1, NA_WIDTH)), full((1, NA_WIDTH))],
        out_specs=[blk(0), blk(0), blk(0), blk(0)],
        out_shape=[jax.ShapeDtypeStruct((n_tok, NA_WIDTH), BF16)] * 3 + [jax.ShapeDtypeStruct((n_tok, NA_WIDTH), F32)],
        compiler_params=_cparams("parallel"),
        name="qkv_prep",
    )(proj, proj, proj, ones_bd, gq, gk)


_NT = (((1,), (1,)), ((), ()))


def _ctx_attn_kernel(q_ref, k_ref, v_ref, o_ref):
    scale = NA_HD ** -0.5
    outs = []
    for h in range(NA_HEADS):
        hs = slice(h * NA_HD, (h + 1) * NA_HD)
        s = lax.dot_general(q_ref[:, hs], k_ref[:, hs], _NT, preferred_element_type=F32) * scale
        m = jnp.max(s, axis=-1, keepdims=True)
        e = jnp.exp(s - m)
        l = jnp.sum(e, axis=-1, keepdims=True)
        o = jnp.dot(e.astype(BF16), v_ref[:, hs], preferred_element_type=F32)
        outs.append(o / l)
        if h % 2 == 1:
            o_ref[:, (h - 1) * NA_HD:(h + 1) * NA_HD] = jnp.concatenate(outs, axis=1).astype(o_ref.dtype)
            outs = []


def ctx_attention(q, k, v, n_seq, seq_len):
    blk = pl.BlockSpec((seq_len, NA_WIDTH), lambda b: (b, 0))
    return pl.pallas_call(
        _ctx_attn_kernel,
        grid=(n_seq,),
        in_specs=[blk, blk, blk],
        out_specs=blk,
        out_shape=jax.ShapeDtypeStruct((n_seq * seq_len, NA_WIDTH), BF16),
        compiler_params=_cparams("parallel"),
        name="ctx_attn",
    )(q, k, v)


def _na_attn_kernel(q_ref, k_ref, v_ref, kc_ref, vc_ref, bias_ref, o_ref, *, rows):
    r = pl.program_id(1)
    rs = jnp.clip(r - NA_WIN_R // 2, 0, rows - NA_WIN_R)
    k0 = pl.multiple_of(rs * GRID_W, GRID_W)
    nloc = NA_WIN_R * GRID_W
    scale = NA_HD ** -0.5
    outs = []
    for h in range(NA_HEADS):
        hs = slice(h * NA_HD, (h + 1) * NA_HD)
        qh = q_ref[:, hs]
        s_loc = lax.dot_general(qh, k_ref[pl.ds(k0, nloc), hs], _NT, preferred_element_type=F32) * scale + bias_ref[h]
        s_ctx = lax.dot_general(qh, kc_ref[:, hs], _NT, preferred_element_type=F32) * scale
        m = jnp.maximum(jnp.max(s_loc, axis=-1, keepdims=True), jnp.max(s_ctx, axis=-1, keepdims=True))
        e_loc = jnp.exp(s_loc - m)
        e_ctx = jnp.exp(s_ctx - m)
        l = jnp.sum(e_loc, axis=-1, keepdims=True) + jnp.sum(e_ctx, axis=-1, keepdims=True)
        o = (jnp.dot(e_loc.astype(BF16), v_ref[pl.ds(k0, nloc), hs], preferred_element_type=F32)
             + jnp.dot(e_ctx.astype(BF16), vc_ref[:, hs], preferred_element_type=F32))
        outs.append(o / l)
        if h % 2 == 1:
            o_ref[:, (h - 1) * NA_HD:(h + 1) * NA_HD] = jnp.concatenate(outs, axis=1).astype(o_ref.dtype)
            outs = []


def na_bias_table(rpb):
    qc = jnp.arange(GRID_W)
    kc = jnp.arange(GRID_W)
    cs = jnp.clip(qc - NA_WIN_C // 2, 0, GRID_W - NA_WIN_C)
    valid = (kc[None, :] >= cs[:, None]) & (kc[None, :] < cs[:, None] + NA_WIN_C)
    colrel = jnp.clip(kc[None, :] - qc[:, None] + NA_WIN_C - 1, 0, 2 * NA_WIN_C - 2)
    cls = jnp.arange(NA_WIN_R)
    j = jnp.arange(NA_WIN_R)
    rowrel = j[None, :] - cls[:, None] + NA_WIN_R - 1
    tab = rpb[:, rowrel][:, :, :, colrel]
    tab = jnp.where(valid[None, None, None], tab, NEG_BIG)
    tab = jnp.transpose(tab, (0, 1, 3, 2, 4))
    return tab.reshape(rpb.shape[0], NA_WIN_R, GRID_W, NA_WIN_R * GRID_W)


def na_attention(q, k, v, k_ctx, v_ctx, bias_tab, tok0, n_seq, seq_len):
    rows = seq_len // GRID_W
    past = k_ctx.shape[1]
    assert tok0 % seq_len == 0 and rows >= NA_WIN_R
    seq0 = tok0 // seq_len
    row0 = tok0 // GRID_W
    cls = lambda r: r - jnp.clip(r - NA_WIN_R // 2, 0, rows - NA_WIN_R)
    return pl.pallas_call(
        functools.partial(_na_attn_kernel, rows=rows),
        grid=(n_seq, rows),
        in_specs=[pl.BlockSpec((GRID_W, NA_WIDTH), lambda b, r: (row0 + b * rows + r, 0)),
                  pl.BlockSpec((seq_len, NA_WIDTH), lambda b, r: (seq0 + b, 0)),
                  pl.BlockSpec((seq_len, NA_WIDTH), lambda b, r: (seq0 + b, 0)),
                  pl.BlockSpec((None, past, NA_WIDTH), lambda b, r: (b, 0, 0)),
                  pl.BlockSpec((None, past, NA_WIDTH), lambda b, r: (b, 0, 0)),
                  pl.BlockSpec((NA_HEADS, None, GRID_W, NA_WIN_R * GRID_W), lambda b, r: (0, cls(r), 0, 0))],
        out_specs=pl.BlockSpec((GRID_W, NA_WIDTH), lambda b, r: (b * rows + r, 0)),
        out_shape=jax.ShapeDtypeStruct((n_seq * seq_len, NA_WIDTH), BF16),
        compiler_params=_cparams("parallel", "arbitrary"),
        name="na_attn",
    )(q, k, v, k_ctx, v_ctx, bias_tab)


class HyPlan:
    def __init__(self, P):
        self.P, self.M = P, 2 * P
        self.K1 = P + 1
        self.K1P = -(-self.K1 // 8) * 8
        K1, K1P, M = self.K1, self.K1P, self.M
        ang = 2.0 * np.pi * ((np.arange(K1)[:, None] * np.arange(P)[None, :]) % M) / M
        ma = np.zeros((2 * K1P, P))
        ma[:K1] = np.cos(ang)
        ma[K1P:K1P + K1] = -np.sin(ang)
        self.ma = ma
        w = np.where((np.arange(K1) == 0) | (np.arange(K1) == P), 1.0, 2.0)[None, :]
        g = np.zeros((P, 2 * K1P))
        g[:, :K1] = w * np.cos(ang.T) / M
        g[:, K1P:K1P + K1] = -w * np.sin(ang.T) / M
        self.g = g
        sgn = np.zeros((K1P, LANES), np.float32)
        sgn[:K1] = np.where(np.arange(K1) % 2 == 0, 1.0, -1.0)[:, None]
        self.sgn = sgn


def _conv3_kernel(x_ref, w_ref, b_ref, o_ref):
    x = x_ref[...]
    L = x.shape[0]
    row = lax.broadcasted_iota(jnp.int32, x.shape, 0)
    xm = jnp.where(row == 0, 0.0, pltpu.roll(x, 1, axis=0))
    xp = jnp.where(row == L - 1, 0.0, pltpu.roll(x, L - 1, axis=0))
    o_ref[...] = b_ref[...] + w_ref[0:1, :] * xm + w_ref[1:2, :] * x + w_ref[2:3, :] * xp


def hy_conv3(proj3, seq0, n_seq, col0, conv_w, conv_b, width):
    _, L, _ = proj3.shape
    ct = 256
    nj = width // ct
    return pl.pallas_call(
        _conv3_kernel,
        grid=(n_seq, 3, nj),
        in_specs=[pl.BlockSpec((None, L, ct), lambda b, s, j: (seq0 + b, 0, (col0 + s) * nj + j)),
                  pl.BlockSpec((HY_SHORT, ct), lambda b, s, j: (0, s * nj + j)),
                  pl.BlockSpec((1, ct), lambda b, s, j: (0, s * nj + j))],
        out_specs=pl.BlockSpec((None, None, L, ct), lambda b, s, j: (s, b, 0, j)),
        out_shape=jax.ShapeDtypeStruct((3, n_seq, L, width), F32),
        compiler_params=_cparams("parallel", "parallel", "parallel"),
        name="hy_conv3",
    )(proj3, conv_w, conv_b.reshape(1, -1))


def _hy_filter_kernel(z_ref, t_ref, w1_ref, b1_ref, w2_ref, b2_ref, fr_ref, w3f_ref, w3b_ref, dec_ref, o_ref):
    fr = fr_ref[...]
    h = jnp.sin(fr * (_dot3_r(z_ref[...], *_split_bf16(w1_ref[...])) + b1_ref[...]))
    h = jnp.sin(fr * (_dot3_r(h, *_split_bf16(w2_ref[...])) + b2_ref[...]))
    h_hi, h_lo = _split_bf16(h)
    ct = w3f_ref.shape[1]
    t01 = jnp.concatenate([t_ref[...]] * (ct // LANES), axis=1)
    row = lax.broadcasted_iota(jnp.int32, t01.shape, 0)

    def branch(w3_ref, dec):
        w_hi, w_lo = _split_bf16(w3_ref[...])
        hv = (jnp.dot(h_hi, w_hi, preferred_element_type=F32) + jnp.dot(h_hi, w_lo, preferred_element_type=F32)
              + jnp.dot(h_lo, w_hi, preferred_element_type=F32))
        return hv * jnp.exp(-t01 * jnp.abs(dec))

    hf = branch(w3f_ref, dec_ref[0])
    hb = jnp.where(row == 0, 0.0, branch(w3b_ref, dec_ref[1]))
    norm = jnp.sum(jnp.abs(hf), axis=0, keepdims=True) + jnp.sum(jnp.abs(hb), axis=0, keepdims=True)
    inv = 1.0 / norm
    o_ref[0] = hf * inv
    o_ref[1] = hb * inv


def hy_filters(L, w1, b1, w2, b2, w3, freq, decay, width):
    t = jnp.arange(L, dtype=F32)
    t01 = t / L
    bands = jnp.linspace(1e-4, HY_BANDS - 1, HY_BANDS, dtype=F32)
    ang = (2.0 * math.pi / L) * t[:, None] * bands[None, :]
    z = jnp.concatenate([t01[:, None], jnp.cos(ang), -jnp.sin(ang), jnp.zeros((L, HY_EMB_PAD - HY_EMB), F32)], axis=-1)
    w1p = jnp.concatenate([w1, jnp.zeros((HY_EMB_PAD - HY_EMB, HY_FFN), F32)], axis=0)
    ct = 256
    nj = width // ct
    full = lambda s: pl.BlockSpec(s, lambda o, j: (0,) * len(s))
    return pl.pallas_call(
        _hy_filter_kernel,
        grid=(HY_ORDER, nj),
        in_specs=[full((L, HY_EMB_PAD)), full((L, LANES)), full((HY_EMB_PAD, HY_FFN)), full((1, HY_FFN)),
                  full((HY_FFN, HY_FFN)), full((1, HY_FFN)), full((1, HY_FFN)),
                  pl.BlockSpec((HY_FFN, ct), lambda o, j: (0, (2 * o) * nj + j)),
                  pl.BlockSpec((HY_FFN, ct), lambda o, j: (0, (2 * o + 1) * nj + j)),
                  pl.BlockSpec((None, 2, 1, ct), lambda o, j: (o, 0, 0, j))],
        out_specs=pl.BlockSpec((None, 2, L, ct), lambda o, j: (o, 0, 0, j)),
        out_shape=jax.ShapeDtypeStruct((HY_ORDER, 2, L, width), F32),
        compiler_params=_cparams("parallel", "parallel"),
        name="hy_filter",
    )(z, jnp.broadcast_to(t01[:, None], (L, LANES)), w1p, b1.reshape(1, -1), w2, b2.reshape(1, -1),
      freq.reshape(1, -1), w3, w3, decay.reshape(HY_ORDER, 2, 1, width))


def _fwd_dft_kernel(x_ref, mhi_ref, mlo_ref, o_ref):
    o_ref[...] = _dot3(mhi_ref[...], mlo_ref[...], x_ref[...])


def hy_fwd_dft(xb, plan):
    NB, P, C = xb.shape
    mhi, mlo = _const_split(plan.ma)
    return pl.pallas_call(
        _fwd_dft_kernel,
        grid=(NB,),
        in_specs=[pl.BlockSpec((None, P, C), lambda b: (b, 0, 0)),
                  pl.BlockSpec(mhi.shape, lambda b: (0, 0)),
                  pl.BlockSpec(mlo.shape, lambda b: (0, 0))],
        out_specs=pl.BlockSpec((None, 2 * plan.K1P, C), lambda b: (b, 0, 0)),
        out_shape=jax.ShapeDtypeStruct((NB, 2 * plan.K1P, C), F32),
        compiler_params=_cparams("parallel"),
        name="hy_fwd_dft",
    )(xb, mhi, mlo)


def _row_chunk(k1p):
    return max(r for r in range(8, 49, 8) if k1p % r == 0)


def _block_mul_kernel(x_ref, s_ref, sgn_ref, y_ref, g_s, *, nb, k1p, ct):
    sgn = jnp.concatenate([sgn_ref[...]] * (ct // LANES), axis=1)
    sgn2 = jnp.concatenate([sgn, sgn], axis=0)
    for d in range(-(nb - 1), nb):
        g_s[d + nb - 1] = s_ref[d + nb] + sgn2 * s_ref[d + nb - 1]

    rc = _row_chunk(k1p)

    def chunk(c, _):
        r0 = pl.multiple_of(c * rc, 8)
        for i in range(nb):
            def body(j, acc):
                yr, yi = acc
                xr = x_ref[j, pl.ds(r0, rc), :]
                xi = x_ref[j, pl.ds(k1p + r0, rc), :]
                gr = g_s[i - j + nb - 1, pl.ds(r0, rc), :]
                gi = g_s[i - j + nb - 1, pl.ds(k1p + r0, rc), :]
                return yr + (xr * gr - xi * gi), yi + (xr * gi + xi * gr)

            zero = jnp.zeros((rc, ct), F32)
            yr, yi = lax.fori_loop(0, nb, body, (zero, zero), unroll=True)
            y_ref[i, pl.ds(r0, rc), :] = yr
            y_ref[i, pl.ds(k1p + r0, rc), :] = yi
        return 0

    lax.fori_loop(0, k1p // rc, chunk, 0)


def hy_block_mul(x, s, plan, nb, order):
    NBt, k2, C = x.shape
    B = NBt // nb
    ct = LANES
    nj = C // ct
    return pl.pallas_call(
        functools.partial(_block_mul_kernel, nb=nb, k1p=plan.K1P, ct=ct),
        grid=(nj, B),
        in_specs=[pl.BlockSpec((nb, k2, ct), lambda j, b: (b, 0, j)),
                  pl.BlockSpec((2 * nb, k2, ct), lambda j, b: (order, 0, j)),
                  pl.BlockSpec((plan.K1P, LANES), lambda j, b: (0, 0))],
        out_specs=pl.BlockSpec((nb, k2, ct), lambda j, b: (b, 0, j)),
        out_shape=jax.ShapeDtypeStruct((NBt, k2, C), F32),
        scratch_shapes=[pltpu.VMEM((2 * nb - 1, k2, ct), F32)],
        compiler_params=_cparams("parallel", "arbitrary"),
        name="hy_block_mul",
    )(x, s, jnp.asarray(plan.sgn))


def _inv_gate_kernel(y_ref, ghi_ref, glo_ref, z_ref, gate_ref, bias_ref, o_ref):
    conv = _dot3(ghi_ref[...], glo_ref[...], y_ref[...])
    o_ref[...] = (gate_ref[...] * (conv + bias_ref[...] * z_ref[...])).astype(o_ref.dtype)


def hy_inv_gate(y, plan, zb, u, gate_idx, bias_n, out_dtype):
    NB, P, C = zb.shape
    ghi, glo = _const_split(plan.g)
    tok = pl.BlockSpec((None, P, C), lambda b: (b, 0, 0))
    return pl.pallas_call(
        _inv_gate_kernel,
        grid=(NB,),
        in_specs=[pl.BlockSpec((None, 2 * plan.K1P, C), lambda b: (b, 0, 0)),
                  pl.BlockSpec(ghi.shape, lambda b: (0, 0)), pl.BlockSpec(glo.shape, lambda b: (0, 0)),
                  tok, pl.BlockSpec((None, None, P, C), lambda b: (gate_idx, b, 0, 0)),
                  pl.BlockSpec((1, C), lambda b: (0, 0))],
        out_specs=tok,
        out_shape=jax.ShapeDtypeStruct((NB, P, C), out_dtype),
        compiler_params=_cparams("parallel"),
        name="hy_inv_gate",
    )(y, ghi, glo, zb, u, bias_n.reshape(1, C))


def hyena_mixer(proj3, seq0, n_seq, col0, conv_w, conv_b, w1, b1, w2, b2, w3, freq, decay, bias):
    _, L, _ = proj3.shape
    width, out_dtype = HY_WIDTH, BF16
    P = min(L, HY_BLOCK)
    nb = L // P
    plan = HyPlan(P)
    u = hy_conv3(proj3, seq0, n_seq, col0, conv_w, conv_b, width).reshape(3, n_seq * nb, P, width)
    filt = hy_filters(L, w1, b1, w2, b2, w3, freq, decay, width)
    klin = jnp.concatenate([jnp.roll(jnp.flip(filt[:, 1], axis=1), 1, axis=1), filt[:, 0]], axis=1)
    s = hy_fwd_dft(klin.reshape(HY_ORDER * 2 * nb, P, width), plan)
    z = u[0]
    for n in range(HY_ORDER):
        x = hy_fwd_dft(z, plan)
        y = hy_block_mul(x, s, plan, nb, n)
        z = hy_inv_gate(y, plan, z, u, 1 + n, bias[n], out_dtype if n == HY_ORDER - 1 else F32)
    return z.reshape(n_seq, L, width)


def _time_major(x3, n_cols):
    return jnp.transpose(x3[..., :n_cols], (1, 0, 2))


def _batch_major(y_rows, L, B):
    return jnp.transpose(y_rows.reshape(L, B, -1), (1, 0, 2)).reshape(B * L, -1)


def _segments(B):
    assert B % CHAINS == 0 or CHAINS % B == 0
    return 1 if B % CHAINS == 0 else CHAINS // B


def kernel(x_prompt, x_sample, cache_na_k, cache_na_v, state_s5, state_rglru, c, c_ctx, norm_g, mod_w, mod_b, ffn_wg, ffn_wu, ffn_wd, even_w_in, even_w_out, s5_lam_re, s5_lam_im, s5_log_dt, s5_b_re, s5_b_im, s5_c_re, s5_c_im, s5_d, s5_glu_w, s5_glu_b, hy_conv_w, hy_conv_b, hy_w1, hy_b1, hy_w2, hy_b2, hy_w3, hy_freq, hy_decay, hy_bias, odd_w_in, odd_w_out, rg_conv_w, rg_conv_b, rg_wa, rg_ba, rg_wi, rg_bi, rg_lam, na_qn, na_kn, na_rpb):
    bp, lp, _ = x_prompt.shape
    bs, ls, _ = x_sample.shape
    n_p = bp * lp
    n_s = bs * ls
    n_tok = n_p + n_s
    assert n_p % ls == 0
    groups = ((0, n_p, bp, lp), (n_p, n_s, bs, ls))

    cond = jnp.concatenate([c_ctx[None], c, jnp.zeros((COND_ROWS - 1 - bs, D_MODEL), F32)], axis=0)
    m_all = adaln_all(cond, mod_w, mod_b)

    wg, wu, wd = ffn_weights(ffn_wg, ffn_wu, ffn_wd)
    w_in_e = even_w_in.astype(BF16)
    w_out_e = even_w_out.astype(BF16)
    w_in_o = odd_w_in.astype(BF16)
    w_out_o = odd_w_out.astype(BF16)

    x = jnp.concatenate([x_prompt.reshape(n_p, D_MODEL), x_sample.reshape(n_s, D_MODEL)], axis=0)
    s5_new, rg_new, k_new, v_new = [], [], [], []
    for li in range(DEPTH):
        x, h = ffn_block(x, m_all, norm_g, wg, wu, wd, li, 0, n_p, ls)
        if li % 2 == 0:
            ei = li // 2
            proj = in_proj(h, w_in_e, ei)
            prm = s5_params(s5_lam_re[ei], s5_lam_im[ei], s5_log_dt[ei], s5_b_re[ei], s5_b_im[ei],
                            s5_c_re[ei], s5_c_im[ei])
            ya, yb = [], []
            for gi, (t0, nt, B, L) in enumerate(groups):
                S = _segments(B)
                nc, Ls = B * S, L // S
                u_tm = _time_major(proj[t0:t0 + nt].reshape(nc, Ls, -1), S5_WIDTH)
                if gi == 0:
                    y, hfin = s5_scan(u_tm, *prm, jnp.zeros((2, nc, 2 * S5_STATE), F32))
                    fin = hfin.reshape(2, B, 2, S5_GROUPS, S5_P)
                    s5_new.append(jnp.transpose(fin, (1, 0, 3, 4, 2)))
                else:
                    h0 = jnp.transpose(state_s5[:, ei], (1, 0, 4, 2, 3)).reshape(2, B, 2 * S5_STATE)
                    y = s5_segmented(u_tm, prm, h0, B, S)
                out = s5_glu(u_tm.reshape(Ls * nc, S5_WIDTH), y.reshape(2, Ls * nc, S5_WIDTH),
                             s5_d[ei].reshape(-1), s5_glu_w[ei], s5_glu_b[ei])
                ya.append(_batch_major(out, Ls, nc))
                yb.append(hyena_mixer(proj.reshape(n_tok // L, L, -1), t0 // L, B, 1, hy_conv_w[ei], hy_conv_b[ei],
                                      hy_w1[ei], hy_b1[ei], hy_w2[ei], hy_b2[ei], hy_w3[ei], hy_freq[ei],
                                      hy_decay[ei], hy_bias[ei]).reshape(nt, HY_WIDTH))
            x = out_proj_residual(jnp.concatenate(ya, axis=0), jnp.concatenate(yb, axis=0), w_out_e, ei, x, m_all,
                                  li, n_p, ls)
        else:
            oi = li // 2
            proj = in_proj(h, w_in_o, oi)
            gprm = rglru_params(rg_wa[oi], rg_ba[oi], rg_wi[oi], rg_bi[oi])
            q, k, v, k_f32 = qkv_prep(proj, 2, na_qn[oi], na_kn[oi])
            yc, yd = [], []
            for gi, (t0, nt, B, L) in enumerate(groups):
                S = _segments(B)
                nc, Ls = B * S, L // S
                p3 = proj[t0:t0 + nt].reshape(B, L, -1)
                x_tm = _time_major(p3.reshape(nc, Ls, -1), 2 * RG_WIDTH)
                if gi == 0:
                    h0 = jnp.zeros((2, B, RG_WIDTH), F32)
                else:
                    h0 = jnp.transpose(state_rglru[:, oi], (1, 0, 2))
                y_tm, hfin = rglru_mixer_tm(x_tm, rg_conv_w[oi], rg_conv_b[oi], gprm, rg_lam[oi], h0, B, S)
                yc.append(_batch_major(y_tm.reshape(Ls * nc, RG_WIDTH), Ls, nc))
                if gi == 0:
                    rg_new.append(jnp.transpose(hfin, (1, 0, 2)))
                    yd.append(ctx_attention(q, k, v, B, L))
                    k_new.append(k_f32[:nt].reshape(B, L, NA_HEADS, NA_HD))
                    v_new.append(p3[..., 4 * NA_WIDTH:].reshape(B, L, NA_HEADS, NA_HD))
                else:
                    past = cache_na_k.shape[2]
                    kc = cache_na_k[:, oi].reshape(B, past, NA_WIDTH).astype(BF16)
                    vc = cache_na_v[:, oi].reshape(B, past, NA_WIDTH).astype(BF16)
                    yd.append(na_attention(q, k, v, kc, vc, na_bias_table(na_rpb[oi]), t0, B, L))
            x = out_proj_residual(jnp.concatenate(yc, axis=0), jnp.concatenate(yd, axis=0), w_out_o, oi, x, m_all,
                                  li, n_p, ls)
        (x,) = ffn_block(x, m_all, norm_g, wg, wu, wd, li, 1, n_p, ls)

    y_prompt = x[:n_p].reshape(bp, lp, D_MODEL)
    y_sample = x[n_p:].reshape(bs, ls, D_MODEL)
    new_na_k = jnp.stack(k_new, axis=1)
    new_na_v = jnp.stack(v_new, axis=1)
    new_s5 = jnp.stack(s5_new, axis=1)
    new_rglru = jnp.stack(rg_new, axis=1)
    return (y_prompt, y_sample, new_na_k, new_na_v, new_s5, new_rglru)
```

```python
import functools
import math

import numpy as np
import jax
import jax.numpy as jnp
from jax import lax
from jax.experimental import pallas as pl
from jax.experimental.pallas import tpu as pltpu

F32 = jnp.float32
BF16 = jnp.bfloat16

D_MODEL = 2048
DEPTH = 4
N_MOD = 9
D_FF = 5632
EPS = 1e-6
MIX_HALF = D_MODEL // 2
GRID_W = 64
S5_WIDTH = MIX_HALF
S5_H = 16
S5_GROUPS = S5_WIDTH // S5_H
S5_P = 64
S5_STATE = S5_GROUPS * S5_P
S5_LAM_RE_MAX = -1e-4
HY_WIDTH = MIX_HALF
HY_ORDER = 2
HY_SHORT = 3
HY_BANDS = 16
HY_EMB = 1 + 2 * HY_BANDS
HY_EMB_PAD = 40
HY_FFN = 64
RG_WIDTH = MIX_HALF
RG_BLOCKS = 16
RG_BW = RG_WIDTH // RG_BLOCKS
RG_CONV = 4
RG_LEFT = RG_CONV // 2
RG_C = 8.0
NA_HD = 64
NA_HEADS = MIX_HALF // NA_HD
NA_WIDTH = NA_HEADS * NA_HD
NA_WIN_R = 8
NA_WIN_C = 16

VMEM_LIMIT_BYTES = 56 * 1024 * 1024
LANES = 128
CHAINS = 8
COND_ROWS = 8
TOKEN_TILE = 512
FF_TILE = 512
SCAN_T = 64
S5_COLS = 512
S5_BBLK = 256
S5_CBLK = 128
RG_GBLK = 256
HEAD_CHUNK = 256
NEG_BIG = -1e30
NA_QROWS = 4
NA_KROWS = NA_QROWS + NA_WIN_R
HY_BLOCK = 512
HY_CONV_TILE_ELEMS = 1024 * 1024
HY_MUL_LANES = 1024
HY_MUL_ACC_VREGS = 5


def _cparams(*sem):
    return pltpu.CompilerParams(dimension_semantics=sem, vmem_limit_bytes=VMEM_LIMIT_BYTES)


def _split_bf16(x):
    hi = x.astype(BF16)
    lo = (x - hi.astype(F32)).astype(BF16)
    return hi, lo


def _dot3(a_hi, a_lo, x):
    x_hi, x_lo = _split_bf16(x)
    return (jnp.dot(a_hi, x_hi, preferred_element_type=F32) + jnp.dot(a_lo, x_hi, preferred_element_type=F32)
            + jnp.dot(a_hi, x_lo, preferred_element_type=F32))


def _dot3_r(x, b_hi, b_lo):
    x_hi, x_lo = _split_bf16(x)
    return (jnp.dot(x_hi, b_hi, preferred_element_type=F32) + jnp.dot(x_hi, b_lo, preferred_element_type=F32)
            + jnp.dot(x_lo, b_hi, preferred_element_type=F32))


def _const_split(m):
    return _split_bf16(jnp.asarray(np.asarray(m, np.float32)))


def _gelu_tanh(x):
    return 0.5 * x * (1.0 + jnp.tanh(math.sqrt(2.0 / math.pi) * (x + 0.044715 * (x * x * x))))


def _adaln_kernel(cond_ref, w_ref, b_ref, o_ref):
    c = cond_ref[...]
    s = (c * jax.nn.sigmoid(c)).astype(BF16)
    o_ref[...] = jnp.dot(s, w_ref[...].astype(BF16), preferred_element_type=F32) + b_ref[...]


def adaln_all(cond, mod_w, mod_b):
    tn = 1024
    n_out = N_MOD * D_MODEL
    out = pl.pallas_call(
        _adaln_kernel,
        grid=(DEPTH, n_out // tn),
        in_specs=[
            pl.BlockSpec((COND_ROWS, D_MODEL), lambda l, j: (0, 0)),
            pl.BlockSpec((None, D_MODEL, tn), lambda l, j: (l, 0, j)),
            pl.BlockSpec((None, 1, tn), lambda l, j: (l, 0, j)),
        ],
        out_specs=pl.BlockSpec((None, COND_ROWS, tn), lambda l, j: (l, 0, j)),
        out_shape=jax.ShapeDtypeStruct((DEPTH, COND_ROWS, n_out), F32),
        compiler_params=_cparams("parallel", "parallel"),
        name="adaln",
    )(cond, mod_w, mod_b.reshape(DEPTH, 1, n_out))
    return out.reshape(DEPTH, COND_ROWS, N_MOD, D_MODEL)


def _rms_modulate(x, g, shift, scale):
    ms = jnp.mean(x * x, axis=-1, keepdims=True)
    return x * lax.rsqrt(ms + EPS) * g * (1.0 + scale) + shift


def _ffn_kernel(x_ref, m_ref, g_ref, wg_ref, wu_ref, wd_ref, o_ref, *rest,
                n_ff, in_rows, gate_row, g_row, next_rows, next_g_row):
    hn_ref = rest[0] if next_rows is not None else None
    h_s, acc_s = rest[-2:]
    k = pl.program_id(1)

    @pl.when(k == 0)
    def _():
        h = _rms_modulate(x_ref[...], g_ref[g_row:g_row + 1, :],
                          m_ref[in_rows[0]:in_rows[0] + 1, :], m_ref[in_rows[1]:in_rows[1] + 1, :])
        h_s[...] = h.astype(BF16)
        acc_s[...] = jnp.zeros_like(acc_s)

    h = h_s[...]
    gt = jnp.dot(h, wg_ref[...], preferred_element_type=F32)
    ut = jnp.dot(h, wu_ref[...], preferred_element_type=F32)
    a = (gt * jax.nn.sigmoid(gt) * ut).astype(BF16)
    acc_s[...] += jnp.dot(a, wd_ref[...], preferred_element_type=F32)

    @pl.when(k == n_ff - 1)
    def _():
        xn = x_ref[...] + 0.5 * m_ref[gate_row:gate_row + 1, :] * acc_s[...]
        o_ref[...] = xn
        if hn_ref is not None:
            hn = _rms_modulate(xn, g_ref[next_g_row:next_g_row + 1, :],
                               m_ref[next_rows[0]:next_rows[0] + 1, :], m_ref[next_rows[1]:next_rows[1] + 1, :])
            hn_ref[...] = hn.astype(BF16)


def _cond_row(i, tm, n_prompt_tok, sample_len):
    return jnp.where(i * tm < n_prompt_tok, 0, 1 + (i * tm - n_prompt_tok) // sample_len)


def ffn_block(x, m_all, norm_g, wg, wu, wd, li, half, n_prompt_tok, sample_len):
    n_tok = x.shape[0]
    tm, tf = TOKEN_TILE, FF_TILE
    n_ff = D_FF // tf
    if half == 0:
        cfg = dict(in_rows=(0, 1), gate_row=2, g_row=0, next_rows=(3, 4), next_g_row=1)
    else:
        cfg = dict(in_rows=(6, 7), gate_row=8, g_row=2, next_rows=None, next_g_row=None)
    row = functools.partial(_cond_row, tm=tm, n_prompt_tok=n_prompt_tok, sample_len=sample_len)
    tok = pl.BlockSpec((tm, D_MODEL), lambda i, k: (i, 0))
    out_specs = [tok]
    out_shape = [jax.ShapeDtypeStruct((n_tok, D_MODEL), F32)]
    if half == 0:
        out_specs.append(tok)
        out_shape.append(jax.ShapeDtypeStruct((n_tok, D_MODEL), BF16))
    return pl.pallas_call(
        functools.partial(_ffn_kernel, n_ff=n_ff, **cfg),
        grid=(n_tok // tm, n_ff),
        in_specs=[
            tok,
            pl.BlockSpec((None, None, N_MOD, D_MODEL), lambda i, k: (li, row(i), 0, 0)),
            pl.BlockSpec((None, 3, D_MODEL), lambda i, k: (li, 0, 0)),
            pl.BlockSpec((None, None, D_MODEL, tf), lambda i, k: (li, half, 0, k)),
            pl.BlockSpec((None, None, D_MODEL, tf), lambda i, k: (li, half, 0, k)),
            pl.BlockSpec((None, None, tf, D_MODEL), lambda i, k: (li, half, k, 0)),
        ],
        out_specs=out_specs,
        out_shape=out_shape,
        scratch_shapes=[pltpu.VMEM((tm, D_MODEL), BF16), pltpu.VMEM((tm, D_MODEL), F32)],
        compiler_params=_cparams("parallel", "arbitrary"),
        name=f"ffn_l{li}_h{half}",
    )(x, m_all, norm_g, wg, wu, wd)


def _proj_kernel(a_ref, w_ref, o_ref):
    o_ref[...] = jnp.dot(a_ref[...], w_ref[...], preferred_element_type=F32).astype(o_ref.dtype)


def in_proj(h, w, idx):
    n_tok = h.shape[0]
    n_out = w.shape[-1]
    tm, tn = TOKEN_TILE, n_out // 2
    return pl.pallas_call(
        _proj_kernel,
        grid=(n_out // tn, n_tok // tm),
        in_specs=[pl.BlockSpec((tm, D_MODEL), lambda j, i: (i, 0)),
                  pl.BlockSpec((None, D_MODEL, tn), lambda j, i: (idx, 0, j))],
        out_specs=pl.BlockSpec((tm, tn), lambda j, i: (i, j)),
        out_shape=jax.ShapeDtypeStruct((n_tok, n_out), F32),
        compiler_params=_cparams("parallel", "parallel"),
        name="in_proj",
    )(h, w)


def _out_proj_kernel(ya_ref, yb_ref, wa_ref, wb_ref, x_ref, m_ref, o_ref):
    y = (jnp.dot(ya_ref[...], wa_ref[...], preferred_element_type=F32)
         + jnp.dot(yb_ref[...], wb_ref[...], preferred_element_type=F32))
    o_ref[...] = x_ref[...] + m_ref[5:6, :] * y


def out_proj_residual(ya, yb, w, idx, x, m_all, li, n_prompt_tok, sample_len):
    n_tok = x.shape[0]
    tm = TOKEN_TILE
    row = functools.partial(_cond_row, tm=tm, n_prompt_tok=n_prompt_tok, sample_len=sample_len)
    return pl.pallas_call(
        _out_proj_kernel,
        grid=(n_tok // tm,),
        in_specs=[pl.BlockSpec((tm, MIX_HALF), lambda i: (i, 0)),
                  pl.BlockSpec((tm, MIX_HALF), lambda i: (i, 0)),
                  pl.BlockSpec((None, MIX_HALF, D_MODEL), lambda i: (idx, 0, 0)),
                  pl.BlockSpec((None, MIX_HALF, D_MODEL), lambda i: (idx, 1, 0)),
                  pl.BlockSpec((tm, D_MODEL), lambda i: (i, 0)),
                  pl.BlockSpec((None, None, N_MOD, D_MODEL), lambda i: (li, row(i), 0, 0))],
        out_specs=pl.BlockSpec((tm, D_MODEL), lambda i: (i, 0)),
        out_shape=jax.ShapeDtypeStruct((n_tok, D_MODEL), F32),
        compiler_params=_cparams("parallel"),
        name="out_proj",
    )(ya, yb, w, w, x, m_all)


def _s5_scan_kernel(u_ref, bw_ref, a_ref, cw_ref, h0_ref, *rest, n_t, t_tile, with_y):
    if with_y:
        y_ref, hfin_ref, br_s, bi_s, hr_s, hi_s = rest
    else:
        hfin_ref, br_s, bi_s, hr_s, hi_s = rest
    d = pl.program_id(0)
    jt = pl.program_id(2)
    rows = t_tile * CHAINS

    @pl.when(jt == 0)
    def _():
        hr_s[...] = h0_ref[:, :S5_STATE]
        hi_s[...] = h0_ref[:, S5_STATE:]

    u = jnp.swapaxes(u_ref[...], 0, 1).reshape(rows, S5_WIDTH).astype(BF16)
    n_bblk = S5_WIDTH // S5_BBLK
    ncol = S5_STATE // n_bblk
    for blk in range(n_bblk):
        bu = jnp.dot(u[:, blk * S5_BBLK:(blk + 1) * S5_BBLK], bw_ref[blk], preferred_element_type=F32)
        br_s[:, blk * ncol:(blk + 1) * ncol] = bu[:, :ncol]
        bi_s[:, blk * ncol:(blk + 1) * ncol] = bu[:, ncol:]

    for cc in range(S5_STATE // S5_COLS):
        cs = slice(cc * S5_COLS, (cc + 1) * S5_COLS)
        ar = a_ref[0, :, cs]
        ai = a_ref[1, :, cs]

        def step(i, carry):
            hr, hi = carry
            t = jnp.where(d == 0, i, t_tile - 1 - i)
            r0 = pl.multiple_of(t * CHAINS, CHAINS)
            nhr = ar * hr - ai * hi + br_s[pl.ds(r0, CHAINS), cs]
            nhi = ar * hi + ai * hr + bi_s[pl.ds(r0, CHAINS), cs]
            if with_y:
                br_s[pl.ds(r0, CHAINS), cs] = nhr
                bi_s[pl.ds(r0, CHAINS), cs] = nhi
            return nhr, nhi

        hr, hi = lax.fori_loop(0, t_tile, step, (hr_s[:, cs], hi_s[:, cs]), unroll=4)
        hr_s[:, cs] = hr
        hi_s[:, cs] = hi

    if with_y:
        n_cblk = S5_WIDTH // S5_CBLK
        kc = S5_STATE // n_cblk
        for i in range(n_cblk):
            yi = jnp.dot(br_s[:, i * kc:(i + 1) * kc].astype(BF16), cw_ref[0, i], preferred_element_type=F32)
            yi = yi + jnp.dot(bi_s[:, i * kc:(i + 1) * kc].astype(BF16), cw_ref[1, i], preferred_element_type=F32)
            y_ref[:, :, i * S5_CBLK:(i + 1) * S5_CBLK] = jnp.swapaxes(yi.reshape(t_tile, CHAINS, S5_CBLK), 0, 1)

    @pl.when(jt == n_t - 1)
    def _():
        hfin_ref[:, :S5_STATE] = hr_s[...]
        hfin_ref[:, S5_STATE:] = hi_s[...]


def s5_scan(u3, seq0, nc, bw, a_b, cw, h0, with_y=True):
    _, L, _ = u3.shape
    assert seq0 % CHAINS == 0 and nc % CHAINS == 0
    c0 = seq0 // CHAINS
    t_tile = min(SCAN_T, L)
    n_t = L // t_tile
    tix = lambda d, j: jnp.where(d == 0, j, n_t - 1 - j)
    fin_spec = pl.BlockSpec((None, CHAINS, 2 * S5_STATE), lambda d, c, j: (d, c, 0))
    fin_shape = jax.ShapeDtypeStruct((2, nc, 2 * S5_STATE), F32)
    out_specs, out_shape = [fin_spec], [fin_shape]
    if with_y:
        out_specs = [pl.BlockSpec((None, CHAINS, t_tile, S5_WIDTH), lambda d, c, j: (d, c, tix(d, j), 0))] + out_specs
        out_shape = [jax.ShapeDtypeStruct((2, nc, L, S5_WIDTH), F32)] + out_shape
    return pl.pallas_call(
        functools.partial(_s5_scan_kernel, n_t=n_t, t_tile=t_tile, with_y=with_y),
        grid=(2, nc // CHAINS, n_t),
        in_specs=[
            pl.BlockSpec((CHAINS, t_tile, S5_WIDTH), lambda d, c, j: (c0 + c, tix(d, j), 0)),
            pl.BlockSpec((None,) + bw.shape[1:], lambda d, c, j: (d, 0, 0, 0)),
            pl.BlockSpec((None,) + a_b.shape[1:], lambda d, c, j: (d, 0, 0, 0)),
            pl.BlockSpec((None,) + cw.shape[1:], lambda d, c, j: (d, 0, 0, 0, 0)),
            fin_spec,
        ],
        out_specs=out_specs,
        out_shape=out_shape,
        scratch_shapes=[pltpu.VMEM((t_tile * CHAINS, S5_STATE), F32), pltpu.VMEM((t_tile * CHAINS, S5_STATE), F32),
                        pltpu.VMEM((CHAINS, S5_STATE), F32), pltpu.VMEM((CHAINS, S5_STATE), F32)],
        compiler_params=_cparams("parallel", "parallel", "arbitrary"),
        name="s5_scan" if with_y else "s5_state",
    )(u3, bw, a_b, cw, h0)


def _seg_init_kernel(hl_ref, h0_ref, ar_ref, ai_ref, o_ref, *, n_seg, width, cplx):
    cr = h0_ref[:, :width]
    ci = h0_ref[:, width:] if cplx else None
    for k in range(n_seg):
        o_ref[k, :, :width] = cr
        ar = ar_ref[k]
        if cplx:
            o_ref[k, :, width:] = ci
            ai = ai_ref[k]
            cr, ci = (ar * cr - ai * ci + hl_ref[k, :, :width], ar * ci + ai * cr + hl_ref[k, :, width:])
        else:
            cr = ar * cr + hl_ref[k]


def seg_init(hl, h0, ar, ai=None):
    _, S, B, W2 = hl.shape
    cplx = ai is not None
    width = W2 // 2 if cplx else W2
    if ai is None:
        ai = ar
    aspec = pl.BlockSpec((None,) + ar.shape[1:], lambda d: (d, 0, 0, 0))
    return pl.pallas_call(
        functools.partial(_seg_init_kernel, n_seg=S, width=width, cplx=cplx),
        grid=(2,),
        in_specs=[pl.BlockSpec((None, S, B, W2), lambda d: (d, 0, 0, 0)),
                  pl.BlockSpec((None, B, W2), lambda d: (d, 0, 0)), aspec, aspec],
        out_specs=pl.BlockSpec((None, S, B, W2), lambda d: (d, 0, 0, 0)),
        out_shape=jax.ShapeDtypeStruct((2, S, B, W2), F32),
        compiler_params=_cparams("parallel"),
        name="seg_init",
    )(hl, h0, ar, ai)


def _to_proc_order(x, B, S):
    x = jnp.transpose(x.reshape(2, B, S, -1), (0, 2, 1, 3))
    return jnp.stack([x[0], x[1, ::-1]], axis=0)


def _from_proc_order(x, B, S):
    x = jnp.stack([x[0], x[1, ::-1]], axis=0)
    return jnp.transpose(x, (0, 2, 1, 3)).reshape(2, B * S, -1)


def _cpow(ar, ai, n):
    rr, ri = None, None
    br, bi = ar, ai
    while n:
        if n & 1:
            rr, ri = (br, bi) if rr is None else (rr * br - ri * bi, rr * bi + ri * br)
        n >>= 1
        if n:
            br, bi = br * br - bi * bi, 2.0 * br * bi
    return rr, ri


def s5_segmented(u3, seq0, prm, h0, B, S):
    bw, a_b, cw = prm
    Ls = u3.shape[1]
    zeros = jnp.zeros((2, B * S, 2 * S5_STATE), F32)
    (hloc,) = s5_scan(u3, seq0, B * S, bw, a_b, cw, zeros, with_y=False)
    pr, pi = _cpow(a_b[:, 0, 0], a_b[:, 1, 0], Ls)
    rep = lambda p: jnp.broadcast_to(p[:, None, None, :], (2, S, 1, S5_STATE))
    init = seg_init(_to_proc_order(hloc, B, S), h0, rep(pr), rep(pi))
    y, _ = s5_scan(u3, seq0, B * S, bw, a_b, cw, _from_proc_order(init, B, S), with_y=True)
    return y


def s5_params(lam_re, lam_im, log_dt, b_re, b_im, c_re, c_im):
    lr = jnp.minimum(lam_re, S5_LAM_RE_MAX)
    li = lam_im
    dt = jnp.exp(log_dt)[..., None]
    mag = jnp.exp(lr * dt)
    abr, abi = mag * jnp.cos(li * dt), mag * jnp.sin(li * dt)
    den = lr * lr + li * li
    cr = ((abr - 1.0) * lr + abi * li) / den
    ci = (abi * lr - (abr - 1.0) * li) / den
    bbr = cr[..., None] * b_re - ci[..., None] * b_im
    bbi = cr[..., None] * b_im + ci[..., None] * b_re
    n_bblk = S5_WIDTH // S5_BBLK
    gb = S5_GROUPS // n_bblk
    bb = jnp.stack([bbr, bbi], axis=1).reshape(2, 2, n_bblk, gb, S5_P, S5_H)
    bw = jnp.einsum('drbgph,gk->dbghrkp', bb, jnp.eye(gb, dtype=F32)).reshape(2, n_bblk, gb * S5_H, 2 * gb * S5_P)
    n_cblk = S5_WIDTH // S5_CBLK
    gc = S5_GROUPS // n_cblk
    cc = jnp.stack([c_re, -c_im], axis=1).reshape(2, 2, n_cblk, gc, S5_H, S5_P)
    cw = jnp.einsum('drbghp,gk->drbgpkh', cc, jnp.eye(gc, dtype=F32)).reshape(2, 2, n_cblk, gc * S5_P, gc * S5_H)
    a_b = jnp.stack([abr.reshape(2, S5_STATE), abi.reshape(2, S5_STATE)], axis=1)
    a_b = jnp.broadcast_to(a_b[:, :, None, :], (2, 2, CHAINS, S5_STATE))
    return bw.astype(BF16), a_b, cw.astype(BF16)


def _s5_glu_kernel(u_ref, y_ref, dsk_ref, w_ref, b_ref, o_ref):
    y = dsk_ref[...] * u_ref[...] + y_ref[0] + y_ref[1]
    z = _gelu_tanh(y)
    gate = jnp.dot(z.astype(BF16), w_ref[...], preferred_element_type=F32) + b_ref[...]
    o_ref[...] = (z * jax.nn.sigmoid(gate)).astype(o_ref.dtype)


def s5_glu(proj, tok0, y2d, d_skip, glu_w, glu_b):
    R = y2d.shape[1]
    tm = TOKEN_TILE
    r0 = tok0 // tm
    return pl.pallas_call(
        _s5_glu_kernel,
        grid=(R // tm,),
        in_specs=[pl.BlockSpec((tm, S5_WIDTH), lambda i: (r0 + i, 0)),
                  pl.BlockSpec((2, tm, S5_WIDTH), lambda i: (0, i, 0)),
                  pl.BlockSpec((1, S5_WIDTH), lambda i: (0, 0)),
                  pl.BlockSpec((S5_WIDTH, S5_WIDTH), lambda i: (0, 0)),
                  pl.BlockSpec((1, S5_WIDTH), lambda i: (0, 0))],
        out_specs=pl.BlockSpec((tm, S5_WIDTH), lambda i: (i, 0)),
        out_shape=jax.ShapeDtypeStruct((R, S5_WIDTH), BF16),
        compiler_params=_cparams("parallel"),
        name="s5_glu",
    )(proj, y2d, d_skip.reshape(1, S5_WIDTH), glu_w.astype(BF16), glu_b.reshape(1, S5_WIDTH))


def _rglru_kernel(xp_ref, xc_ref, xn_ref, cw_ref, cb_ref, wa_ref, ba_ref, wi_ref, bi_ref, lam_ref, h0_ref, *rest,
                  n_t, t_tile, seg, n_seg):
    if seg:
        h_ref, p_ref, hfin_ref, pfin_ref, a_s, b_s, hc_s, ht_s, pc_s, pt_s = rest
    else:
        h_ref, hfin_ref, a_s, b_s, hc_s, ht_s = rest
    d = pl.program_id(0)
    jt = pl.program_id(2)
    tj = jnp.where(d == 0, jt, n_t - 1 - jt)
    rows = t_tile * CHAINS

    @pl.when(jt == 0)
    def _():
        hc_s[...] = h0_ref[...]
        if seg:
            pc_s[...] = jnp.ones_like(pc_s)

    prev_in = jnp.swapaxes(xp_ref[...], 0, 1)[CHAINS - RG_LEFT:]
    next_in = jnp.swapaxes(xn_ref[...], 0, 1)[:RG_CONV - 1 - RG_LEFT]
    if seg:
        cidx = lax.broadcasted_iota(jnp.int32, (1, CHAINS, 1), 1) % n_seg
        first_t = tj == 0
        last_t = tj == n_t - 1
        prev_nb = pltpu.roll(prev_in, 1, axis=1)
        next_nb = pltpu.roll(next_in, CHAINS - 1, axis=1)
        prev = jnp.where(first_t, jnp.where(cidx == 0, 0.0, prev_nb), prev_in)
        nxt = jnp.where(last_t, jnp.where(cidx == n_seg - 1, 0.0, next_nb), next_in)
    else:
        prev = prev_in * (tj > 0).astype(F32)
        nxt = next_in * (tj < n_t - 1).astype(F32)
    xcat = jnp.concatenate([prev, jnp.swapaxes(xc_ref[...], 0, 1), nxt], axis=0)
    xc = cb_ref[...] + cw_ref[0:1, :] * xcat[0:t_tile]
    for j in range(1, RG_CONV):
        xc = xc + cw_ref[j:j + 1, :] * xcat[j:j + t_tile]
    x2 = xc.reshape(rows, RG_WIDTH)
    xb = x2.astype(BF16)

    ra, ia = [], []
    for j in range(RG_WIDTH // RG_GBLK):
        xs = xb[:, j * RG_GBLK:(j + 1) * RG_GBLK]
        ra.append(jnp.dot(xs, wa_ref[j], preferred_element_type=F32))
        ia.append(jnp.dot(xs, wi_ref[j], preferred_element_type=F32))
    r = jax.nn.sigmoid(jnp.concatenate(ra, axis=1) + ba_ref[...])
    ig = jax.nn.sigmoid(jnp.concatenate(ia, axis=1) + bi_ref[...])
    nl = -lam_ref[...]
    sp = jnp.maximum(nl, 0.0) + jnp.log1p(jnp.exp(-jnp.abs(nl)))
    log_a = (-RG_C) * r * sp
    a_s[...] = jnp.exp(log_a)
    th = jnp.tanh(log_a)
    one_minus_a2 = (-2.0) * th / (1.0 - th)
    b_s[...] = jnp.sqrt(one_minus_a2) * (ig * x2)

    def step(i, carry):
        t = jnp.where(d == 0, i, t_tile - 1 - i)
        r0 = pl.multiple_of(t * CHAINS, CHAINS)
        a = a_s[pl.ds(r0, CHAINS), :]
        if seg:
            h, p = carry
            h = a * h + b_s[pl.ds(r0, CHAINS), :]
            p = a * p
            ht_s[t] = h
            pt_s[t] = p
            return h, p
        h = a * carry + b_s[pl.ds(r0, CHAINS), :]
        ht_s[t] = h
        return h

    if seg:
        h, p = lax.fori_loop(0, t_tile, step, (hc_s[...], pc_s[...]), unroll=8)
        pc_s[...] = p
        p_ref[...] = jnp.swapaxes(pt_s[...], 0, 1)
    else:
        h = lax.fori_loop(0, t_tile, step, hc_s[...], unroll=8)
    hc_s[...] = h
    h_ref[...] = jnp.swapaxes(ht_s[...], 0, 1)

    @pl.when(jt == n_t - 1)
    def _():
        hfin_ref[...] = h
        if seg:
            pfin_ref[...] = p


def rglru_scan(x3, seq0, nc, conv_w, conv_b, wa, ba, wi, bi, lam, h0, n_seg=1):
    _, L, _ = x3.shape
    seg = n_seg > 1
    assert seq0 % CHAINS == 0 and nc % CHAINS == 0 and (not seg or nc == CHAINS)
    c0 = seq0 // CHAINS
    t_tile = min(SCAN_T, L)
    n_t = L // t_tile
    tix = lambda d, j: jnp.where(d == 0, j, n_t - 1 - j)
    per = t_tile // CHAINS
    n_hb = L // CHAINS
    if seg:
        prev_ix = lambda d, c, j: (c0 + c, (tix(d, j) * per - 1) % n_hb, 0)
        next_ix = lambda d, c, j: (c0 + c, ((tix(d, j) + 1) * per) % n_hb, 0)
    else:
        prev_ix = lambda d, c, j: (c0 + c, jnp.maximum(tix(d, j) * per - 1, 0), 0)
        next_ix = lambda d, c, j: (c0 + c, jnp.minimum((tix(d, j) + 1) * per, n_hb - 1), 0)
    hspec = pl.BlockSpec((None, CHAINS, t_tile, RG_WIDTH), lambda d, c, j: (d, c, tix(d, j), 0))
    fspec = pl.BlockSpec((None, CHAINS, RG_WIDTH), lambda d, c, j: (d, c, 0))
    hshape = jax.ShapeDtypeStruct((2, nc, L, RG_WIDTH), F32)
    fshape = jax.ShapeDtypeStruct((2, nc, RG_WIDTH), F32)
    tile = pltpu.VMEM((t_tile, CHAINS, RG_WIDTH), F32)
    scratch = [pltpu.VMEM((t_tile * CHAINS, RG_WIDTH), F32), pltpu.VMEM((t_tile * CHAINS, RG_WIDTH), F32),
               pltpu.VMEM((CHAINS, RG_WIDTH), F32), tile]
    if seg:
        out_specs, out_shape = [hspec, hspec, fspec, fspec], [hshape, hshape, fshape, fshape]
        scratch += [pltpu.VMEM((CHAINS, RG_WIDTH), F32), tile]
    else:
        out_specs, out_shape = [hspec, fspec], [hshape, fshape]
    halo = lambda ix: pl.BlockSpec((CHAINS, CHAINS, RG_WIDTH), ix)
    return pl.pallas_call(
        functools.partial(_rglru_kernel, n_t=n_t, t_tile=t_tile, seg=seg, n_seg=n_seg),
        grid=(2, nc // CHAINS, n_t),
        in_specs=[
            halo(prev_ix),
            pl.BlockSpec((CHAINS, t_tile, RG_WIDTH), lambda d, c, j: (c0 + c, tix(d, j), 0)),
            halo(next_ix),
            pl.BlockSpec((RG_CONV, RG_WIDTH), lambda d, c, j: (0, 0)),
            pl.BlockSpec((1, RG_WIDTH), lambda d, c, j: (0, 0)),
            pl.BlockSpec((None,) + wa.shape[1:], lambda d, c, j: (d, 0, 0, 0)),
            pl.BlockSpec((None, 1, RG_WIDTH), lambda d, c, j: (d, 0, 0)),
            pl.BlockSpec((None,) + wi.shape[1:], lambda d, c, j: (d, 0, 0, 0)),
            pl.BlockSpec((None, 1, RG_WIDTH), lambda d, c, j: (d, 0, 0)),
            pl.BlockSpec((None, 1, RG_WIDTH), lambda d, c, j: (d, 0, 0)),
            fspec,
        ],
        out_specs=out_specs,
        out_shape=out_shape,
        scratch_shapes=scratch,
        compiler_params=_cparams("parallel", "parallel", "arbitrary"),
        name="rglru_seg" if seg else "rglru_scan",
    )(x3, x3, x3, conv_w, conv_b.reshape(1, RG_WIDTH), wa, ba, wi, bi, lam.reshape(2, 1, RG_WIDTH), h0)


def rglru_params(w_a, b_a, w_i, b_i):
    nb = RG_WIDTH // RG_GBLK
    per = RG_GBLK // RG_BW
    eye = jnp.eye(per, dtype=F32)

    def bd(w):
        w = w.reshape(2, nb, per, RG_BW, RG_BW)
        return jnp.einsum('dbnce,nm->dbncme', w, eye).reshape(2, nb, RG_GBLK, RG_GBLK).astype(BF16)

    return bd(w_a), b_a.reshape(2, 1, RG_WIDTH), bd(w_i), b_i.reshape(2, 1, RG_WIDTH)


def _rg_combine_kernel(h_ref, g_ref, *rest, seg):
    if seg:
        p_ref, i_ref, o_ref = rest
        hsum = h_ref[0] + p_ref[0] * i_ref[0] + h_ref[1] + p_ref[1] * i_ref[1]
    else:
        (o_ref,) = rest
        hsum = h_ref[0] + h_ref[1]
    o_ref[...] = (hsum * _gelu_tanh(g_ref[...])).astype(o_ref.dtype)


def rg_combine(h2, proj, tok0, col_blk, p2=None, init=None, seg_len=None):
    R = h2.shape[1]
    seg = p2 is not None
    tm = math.gcd(TOKEN_TILE, seg_len) if seg else TOKEN_TILE
    r0 = tok0 // tm
    hspec = pl.BlockSpec((2, tm, RG_WIDTH), lambda i: (0, i, 0))
    in_specs = [hspec, pl.BlockSpec((tm, RG_WIDTH), lambda i: (r0 + i, col_blk))]
    args = [h2, proj]
    if seg:
        in_specs += [hspec, pl.BlockSpec((2, None, 1, RG_WIDTH), lambda i: (0, (i * tm) // seg_len, 0, 0))]
        args += [p2, init.reshape(2, -1, 1, RG_WIDTH)]
    return pl.pallas_call(
        functools.partial(_rg_combine_kernel, seg=seg),
        grid=(R // tm,),
        in_specs=in_specs,
        out_specs=pl.BlockSpec((tm, RG_WIDTH), lambda i: (i, 0)),
        out_shape=jax.ShapeDtypeStruct((R, RG_WIDTH), BF16),
        compiler_params=_cparams("parallel"),
        name="rg_combine",
    )(*args)


def rglru_mixer(proj, tok0, conv_w, conv_b, gprm, lam, h0, B, L, S):
    nc, Ls = B * S, L // S
    x3 = proj.reshape(proj.shape[0] // Ls, Ls, -1)
    seq0 = tok0 // Ls
    if S == 1:
        h, hfin = rglru_scan(x3, seq0, nc, conv_w, conv_b, *gprm, lam, h0)
        return rg_combine(h.reshape(2, nc * Ls, RG_WIDTH), proj, tok0, 1), hfin
    zeros = jnp.zeros((2, nc, RG_WIDTH), F32)
    h, p, hfin, pfin = rglru_scan(x3, seq0, nc, conv_w, conv_b, *gprm, lam, zeros, n_seg=S)
    init = seg_init(_to_proc_order(hfin, B, S), h0, _to_proc_order(pfin, B, S))
    init = _from_proc_order(init, B, S)
    y = rg_combine(h.reshape(2, nc * Ls, RG_WIDTH), proj, tok0, 1, p2=p.reshape(2, nc * Ls, RG_WIDTH), init=init,
                   seg_len=Ls)
    return y, hfin


def _head_rms(x, ones_bd, gain):
    sq = x * x
    hi, lo = _split_bf16(sq)
    parts = []
    for j in range(NA_WIDTH // HEAD_CHUNK):
        cs = slice(j * HEAD_CHUNK, (j + 1) * HEAD_CHUNK)
        parts.append(jnp.dot(hi[:, cs], ones_bd, preferred_element_type=F32)
                     + jnp.dot(lo[:, cs], ones_bd, preferred_element_type=F32))
    ms = jnp.concatenate(parts, axis=1) * (1.0 / NA_HD)
    return x * lax.rsqrt(ms + EPS) * gain


def _qkv_prep_kernel(q_ref, k_ref, v_ref, ones_ref, gq_ref, gk_ref, qo_ref, ko_ref, vo_ref, kf_ref):
    ones_bd = ones_ref[...]
    qo_ref[...] = _head_rms(q_ref[...], ones_bd, gq_ref[...]).astype(BF16)
    kn = _head_rms(k_ref[...], ones_bd, gk_ref[...])
    ko_ref[...] = kn.astype(BF16)
    kf_ref[...] = kn
    vo_ref[...] = v_ref[...].astype(BF16)


def qkv_prep(proj, col0, qn_g, kn_g):
    n_tok = proj.shape[0]
    tm = TOKEN_TILE
    ones_bd = jnp.kron(jnp.eye(HEAD_CHUNK // NA_HD, dtype=F32), jnp.ones((NA_HD, NA_HD), F32)).astype(BF16)
    gq = jnp.tile(qn_g, NA_HEADS).reshape(1, NA_WIDTH)
    gk = jnp.tile(kn_g, NA_HEADS).reshape(1, NA_WIDTH)
    blk = lambda c: pl.BlockSpec((tm, NA_WIDTH), lambda i: (i, c))
    full = lambda s: pl.BlockSpec(s, lambda i: (0, 0))
    return pl.pallas_call(
        _qkv_prep_kernel,
        grid=(n_tok // tm,),
        in_specs=[blk(col0), blk(col0 + 1), blk(col0 + 2), full((HEAD_CHUNK, HEAD_CHUNK)),
                  full((1, NA_WIDTH)), full((1, NA_WIDTH))],
        out_specs=[blk(0), blk(0), blk(0), blk(0)],
        out_shape=[jax.ShapeDtypeStruct((n_tok, NA_WIDTH), BF16)] * 3 + [jax.ShapeDtypeStruct((n_tok, NA_WIDTH), F32)],
        compiler_params=_cparams("parallel"),
        name="qkv_prep",
    )(proj, proj, proj, ones_bd, gq, gk)


_NT = (((1,), (1,)), ((), ()))


def _ctx_attn_kernel(q_ref, k_ref, v_ref, o_ref):
    scale = NA_HD ** -0.5
    outs = []
    for h in range(NA_HEADS):
        hs = slice(h * NA_HD, (h + 1) * NA_HD)
        s = lax.dot_general(q_ref[:, hs], k_ref[:, hs], _NT, preferred_element_type=F32) * scale
        m = jnp.max(s, axis=-1, keepdims=True)
        e = jnp.exp(s - m)
        l = jnp.sum(e, axis=-1, keepdims=True)
        o = jnp.dot(e.astype(BF16), v_ref[:, hs], preferred_element_type=F32)
        outs.append(o / l)
        if h % 2 == 1:
            o_ref[:, (h - 1) * NA_HD:(h + 1) * NA_HD] = jnp.concatenate(outs, axis=1).astype(o_ref.dtype)
            outs = []


def ctx_attention(q, k, v, n_seq, seq_len):
    blk = pl.BlockSpec((seq_len, NA_WIDTH), lambda b: (b, 0))
    return pl.pallas_call(
        _ctx_attn_kernel,
        grid=(n_seq,),
        in_specs=[blk, blk, blk],
        out_specs=blk,
        out_shape=jax.ShapeDtypeStruct((n_seq * seq_len, NA_WIDTH), BF16),
        compiler_params=_cparams("parallel"),
        name="ctx_attn",
    )(q, k, v)


def _na_attn_kernel(q_ref, k_ref, v_ref, kc_ref, vc_ref, bias_ref, o_ref, *, rows):
    rb = pl.program_id(2)
    ws = jnp.clip(rb * NA_QROWS - NA_WIN_R // 2, 0, rows - NA_KROWS)
    k0 = pl.multiple_of(ws * GRID_W, GRID_W)
    nloc = NA_KROWS * GRID_W
    scale = NA_HD ** -0.5
    outs = []
    for h in range(LANES // NA_HD):
        hs = slice(h * NA_HD, (h + 1) * NA_HD)
        qh = q_ref[:, hs]
        s_loc = lax.dot_general(qh, k_ref[pl.ds(k0, nloc), hs], _NT, preferred_element_type=F32) * scale + bias_ref[h]
        s_ctx = lax.dot_general(qh, kc_ref[:, hs], _NT, preferred_element_type=F32) * scale
        m = jnp.maximum(jnp.max(s_loc, axis=-1, keepdims=True), jnp.max(s_ctx, axis=-1, keepdims=True))
        e_loc = jnp.exp(s_loc - m)
        e_ctx = jnp.exp(s_ctx - m)
        l = jnp.sum(e_loc, axis=-1, keepdims=True) + jnp.sum(e_ctx, axis=-1, keepdims=True)
        o = (jnp.dot(e_loc.astype(BF16), v_ref[pl.ds(k0, nloc), hs], preferred_element_type=F32)
             + jnp.dot(e_ctx.astype(BF16), vc_ref[:, hs], preferred_element_type=F32))
        outs.append(o / l)
    o_ref[...] = jnp.concatenate(outs, axis=1).astype(o_ref.dtype)


def na_bias_table(rpb, rows):
    qc = np.arange(GRID_W)
    kc = np.arange(GRID_W)
    cs = np.clip(qc - NA_WIN_C // 2, 0, GRID_W - NA_WIN_C)
    cvalid = (kc[None, :] >= cs[:, None]) & (kc[None, :] < cs[:, None] + NA_WIN_C)
    colrel = np.clip(kc[None, :] - qc[:, None] + NA_WIN_C - 1, 0, 2 * NA_WIN_C - 2)
    toep = rpb[:, :, colrel]
    rowrel = np.zeros((3, NA_QROWS, NA_KROWS), np.int32)
    rvalid = np.zeros((3, NA_QROWS, NA_KROWS), bool)
    for cls, r0 in enumerate((0, NA_WIN_R // 2, rows - NA_QROWS)):
        ws = min(max(r0 - NA_WIN_R // 2, 0), rows - NA_KROWS)
        for qr in range(NA_QROWS):
            r = r0 + qr
            rs = min(max(r - NA_WIN_R // 2, 0), rows - NA_WIN_R)
            for j in range(NA_KROWS):
                kr = ws + j
                rvalid[cls, qr, j] = rs <= kr < rs + NA_WIN_R
                rowrel[cls, qr, j] = min(max(kr - r + NA_WIN_R - 1, 0), 2 * NA_WIN_R - 2)
    tab = toep[:, rowrel]
    valid = rvalid[:, :, :, None, None] & cvalid[None, None, None]
    tab = jnp.where(valid[None], tab, NEG_BIG)
    tab = jnp.transpose(tab, (0, 1, 2, 4, 3, 5))
    return tab.reshape(rpb.shape[0], 3, NA_QROWS * GRID_W, NA_KROWS * GRID_W)


def na_attention(q, k, v, k_ctx, v_ctx, rpb, tok0, n_seq, seq_len):
    rows = seq_len // GRID_W
    past = k_ctx.shape[1]
    qtok = NA_QROWS * GRID_W
    n_rb = rows // NA_QROWS
    assert tok0 % seq_len == 0 and rows % NA_QROWS == 0 and rows >= NA_KROWS
    seq0 = tok0 // seq_len
    qb0 = tok0 // qtok
    hp = LANES // NA_HD
    bias_tab = na_bias_table(rpb, rows)
    cls = lambda rb: jnp.where(rb == 0, 0, jnp.where(rb == n_rb - 1, 2, 1))
    return pl.pallas_call(
        functools.partial(_na_attn_kernel, rows=rows),
        grid=(n_seq, NA_HEADS // hp, n_rb),
        in_specs=[pl.BlockSpec((qtok, LANES), lambda b, h, r: (qb0 + b * n_rb + r, h)),
                  pl.BlockSpec((seq_len, LANES), lambda b, h, r: (seq0 + b, h)),
                  pl.BlockSpec((seq_len, LANES), lambda b, h, r: (seq0 + b, h)),
                  pl.BlockSpec((None, past, LANES), lambda b, h, r: (b, 0, h)),
                  pl.BlockSpec((None, past, LANES), lambda b, h, r: (b, 0, h)),
                  pl.BlockSpec((hp, None, qtok, NA_KROWS * GRID_W), lambda b, h, r: (h, cls(r), 0, 0))],
        out_specs=pl.BlockSpec((qtok, LANES), lambda b, h, r: (b * n_rb + r, h)),
        out_shape=jax.ShapeDtypeStruct((n_seq * seq_len, NA_WIDTH), BF16),
        compiler_params=_cparams("parallel", "parallel", "arbitrary"),
        name="na_attn",
    )(q, k, v, k_ctx, v_ctx, bias_tab)


class HyPlan:
    def __init__(self, P):
        self.P, self.M = P, 2 * P
        self.K1 = P + 1
        self.K1P = -(-self.K1 // 8) * 8
        K1, K1P, M = self.K1, self.K1P, self.M
        ang = 2.0 * np.pi * ((np.arange(K1)[:, None] * np.arange(P)[None, :]) % M) / M
        ma = np.zeros((2 * K1P, P))
        ma[:K1] = np.cos(ang)
        ma[K1P:K1P + K1] = -np.sin(ang)
        self.ma = ma
        w = np.where((np.arange(K1) == 0) | (np.arange(K1) == P), 1.0, 2.0)[None, :]
        g = np.zeros((P, 2 * K1P))
        g[:, :K1] = w * np.cos(ang.T) / M
        g[:, K1P:K1P + K1] = -w * np.sin(ang.T) / M
        self.g = g
        sgn = np.zeros((K1P, LANES), np.float32)
        sgn[:K1] = np.where(np.arange(K1) % 2 == 0, 1.0, -1.0)[:, None]
        self.sgn = sgn


def _conv3_kernel(x_ref, w_ref, b_ref, o_ref):
    x = x_ref[...]
    L = x.shape[0]
    row = lax.broadcasted_iota(jnp.int32, x.shape, 0)
    xm = jnp.where(row == 0, 0.0, pltpu.roll(x, 1, axis=0))
    xp = jnp.where(row == L - 1, 0.0, pltpu.roll(x, L - 1, axis=0))
    o_ref[...] = b_ref[...] + w_ref[0:1, :] * xm + w_ref[1:2, :] * x + w_ref[2:3, :] * xp


def hy_conv3(proj3, seq0, n_seq, col0, conv_w, conv_b, width):
    _, L, _ = proj3.shape
    ct = min(width, max(256, HY_CONV_TILE_ELEMS // L))
    nj = width // ct
    return pl.pallas_call(
        _conv3_kernel,
        grid=(n_seq, 3, nj),
        in_specs=[pl.BlockSpec((None, L, ct), lambda b, s, j: (seq0 + b, 0, (col0 + s) * nj + j)),
                  pl.BlockSpec((HY_SHORT, ct), lambda b, s, j: (0, s * nj + j)),
                  pl.BlockSpec((1, ct), lambda b, s, j: (0, s * nj + j))],
        out_specs=pl.BlockSpec((None, None, L, ct), lambda b, s, j: (s, b, 0, j)),
        out_shape=jax.ShapeDtypeStruct((3, n_seq, L, width), F32),
        compiler_params=_cparams("parallel", "parallel", "parallel"),
        name="hy_conv3",
    )(proj3, conv_w, conv_b.reshape(1, -1))


def _hy_filter_kernel(z_ref, t_ref, w1_ref, b1_ref, w2_ref, b2_ref, fr_ref, w3f_ref, w3b_ref, dec_ref, o_ref):
    fr = fr_ref[...]
    w1 = _split_bf16(w1_ref[...])
    w2 = _split_bf16(w2_ref[...])
    ct = w3f_ref.shape[1]

    def branch(d, w3_ref):
        h = jnp.sin(fr * (_dot3_r(z_ref[d], *w1) + b1_ref[...]))
        h = jnp.sin(fr * (_dot3_r(h, *w2) + b2_ref[...]))
        hv = _dot3_r(h, *_split_bf16(w3_ref[...]))
        t01 = jnp.concatenate([t_ref[d]] * (ct // LANES), axis=1)
        return hv * jnp.exp(-t01 * jnp.abs(dec_ref[d]))

    hf = branch(0, w3f_ref)
    hb = branch(1, w3b_ref)
    row = lax.broadcasted_iota(jnp.int32, hb.shape, 0)
    hb = jnp.where(row == 0, 0.0, hb)
    norm = jnp.sum(jnp.abs(hf), axis=0, keepdims=True) + jnp.sum(jnp.abs(hb), axis=0, keepdims=True)
    inv = 1.0 / norm
    o_ref[0] = hb * inv
    o_ref[1] = hf * inv


def hy_filters(L, w1, b1, w2, b2, w3, freq, decay, width):
    t = jnp.arange(L, dtype=F32)
    t01 = t / L
    bands = jnp.linspace(1e-4, HY_BANDS - 1, HY_BANDS, dtype=F32)
    ang = (2.0 * math.pi / L) * t[:, None] * bands[None, :]
    z = jnp.concatenate([t01[:, None], jnp.cos(ang), -jnp.sin(ang), jnp.zeros((L, HY_EMB_PAD - HY_EMB), F32)], axis=-1)
    rev = (L - jnp.arange(L)) % L
    z2 = jnp.stack([z, z[rev]], axis=0)
    tb = jnp.broadcast_to(t01[:, None], (L, LANES))
    t2 = jnp.stack([tb, tb[rev]], axis=0)
    w1p = jnp.concatenate([w1, jnp.zeros((HY_EMB_PAD - HY_EMB, HY_FFN), F32)], axis=0)
    ct = 256
    nj = width // ct
    full = lambda s: pl.BlockSpec(s, lambda o, j: (0,) * len(s))
    return pl.pallas_call(
        _hy_filter_kernel,
        grid=(HY_ORDER, nj),
        in_specs=[full((2, L, HY_EMB_PAD)), full((2, L, LANES)), full((HY_EMB_PAD, HY_FFN)), full((1, HY_FFN)),
                  full((HY_FFN, HY_FFN)), full((1, HY_FFN)), full((1, HY_FFN)),
                  pl.BlockSpec((HY_FFN, ct), lambda o, j: (0, (2 * o) * nj + j)),
                  pl.BlockSpec((HY_FFN, ct), lambda o, j: (0, (2 * o + 1) * nj + j)),
                  pl.BlockSpec((None, 2, 1, ct), lambda o, j: (o, 0, 0, j))],
        out_specs=pl.BlockSpec((None, 2, L, ct), lambda o, j: (o, 0, 0, j)),
        out_shape=jax.ShapeDtypeStruct((HY_ORDER, 2, L, width), F32),
        compiler_params=_cparams("parallel", "parallel"),
        name="hy_filter",
    )(z2, t2, w1p, b1.reshape(1, -1), w2, b2.reshape(1, -1),
      freq.reshape(1, -1), w3, w3, decay.reshape(HY_ORDER, 2, 1, width))


def _fwd_dft_kernel(x_ref, mhi_ref, mlo_ref, o_ref):
    o_ref[...] = _dot3(mhi_ref[...], mlo_ref[...], x_ref[...])


def hy_fwd_dft(xb, plan):
    NB, P, C = xb.shape
    mhi, mlo = _const_split(plan.ma)
    return pl.pallas_call(
        _fwd_dft_kernel,
        grid=(NB,),
        in_specs=[pl.BlockSpec((None, P, C), lambda b: (b, 0, 0)),
                  pl.BlockSpec(mhi.shape, lambda b: (0, 0)),
                  pl.BlockSpec(mlo.shape, lambda b: (0, 0))],
        out_specs=pl.BlockSpec((None, 2 * plan.K1P, C), lambda b: (b, 0, 0)),
        out_shape=jax.ShapeDtypeStruct((NB, 2 * plan.K1P, C), F32),
        compiler_params=_cparams("parallel"),
        name="hy_fwd_dft",
    )(xb, mhi, mlo)


def _row_chunk(k1p, ct):
    cap = max(8, HY_MUL_ACC_VREGS * 8 * LANES // ct)
    return max(r for r in range(8, cap + 1, 8) if k1p % r == 0)


def _block_mul_kernel(x_ref, s_ref, sgn_ref, y_ref, g_s, *, nb, k1p, ct):
    sgn = jnp.concatenate([sgn_ref[...]] * (ct // LANES), axis=1)
    sgn2 = jnp.concatenate([sgn, sgn], axis=0)
    for d in range(-(nb - 1), nb):
        g_s[d + nb - 1] = s_ref[d + nb] + sgn2 * s_ref[d + nb - 1]

    rc = _row_chunk(k1p, ct)

    def chunk(c, _):
        r0 = pl.multiple_of(c * rc, 8)
        for i in range(nb):
            def body(j, acc):
                yr, yi = acc
                xr = x_ref[j, pl.ds(r0, rc), :]
                xi = x_ref[j, pl.ds(k1p + r0, rc), :]
                gr = g_s[i - j + nb - 1, pl.ds(r0, rc), :]
                gi = g_s[i - j + nb - 1, pl.ds(k1p + r0, rc), :]
                return yr + (xr * gr - xi * gi), yi + (xr * gi + xi * gr)

            zero = jnp.zeros((rc, ct), F32)
            yr, yi = lax.fori_loop(0, nb, body, (zero, zero), unroll=True)
            y_ref[i, pl.ds(r0, rc), :] = yr
            y_ref[i, pl.ds(k1p + r0, rc), :] = yi
        return 0

    lax.fori_loop(0, k1p // rc, chunk, 0)


def hy_block_mul(x, s, plan, nb, order):
    NBt, k2, C = x.shape
    B = NBt // nb
    ct = min(C, max(LANES, HY_MUL_LANES // nb))
    nj = C // ct
    return pl.pallas_call(
        functools.partial(_block_mul_kernel, nb=nb, k1p=plan.K1P, ct=ct),
        grid=(nj, B),
        in_specs=[pl.BlockSpec((nb, k2, ct), lambda j, b: (b, 0, j)),
                  pl.BlockSpec((2 * nb, k2, ct), lambda j, b: (order, 0, j)),
                  pl.BlockSpec((plan.K1P, LANES), lambda j, b: (0, 0))],
        out_specs=pl.BlockSpec((nb, k2, ct), lambda j, b: (b, 0, j)),
        out_shape=jax.ShapeDtypeStruct((NBt, k2, C), F32),
        scratch_shapes=[pltpu.VMEM((2 * nb - 1, k2, ct), F32)],
        compiler_params=_cparams("parallel", "arbitrary"),
        name="hy_block_mul",
    )(x, s, jnp.asarray(plan.sgn))


def _inv_gate_kernel(y_ref, ghi_ref, glo_ref, z_ref, gate_ref, bias_ref, o_ref):
    conv = _dot3(ghi_ref[...], glo_ref[...], y_ref[...])
    o_ref[...] = (gate_ref[...] * (conv + bias_ref[...] * z_ref[...])).astype(o_ref.dtype)


def hy_inv_gate(y, plan, zb, u, gate_idx, bias_n, out_dtype):
    NB, P, C = zb.shape
    ghi, glo = _const_split(plan.g)
    tok = pl.BlockSpec((None, P, C), lambda b: (b, 0, 0))
    return pl.pallas_call(
        _inv_gate_kernel,
        grid=(NB,),
        in_specs=[pl.BlockSpec((None, 2 * plan.K1P, C), lambda b: (b, 0, 0)),
                  pl.BlockSpec(ghi.shape, lambda b: (0, 0)), pl.BlockSpec(glo.shape, lambda b: (0, 0)),
                  tok, pl.BlockSpec((None, None, P, C), lambda b: (gate_idx, b, 0, 0)),
                  pl.BlockSpec((1, C), lambda b: (0, 0))],
        out_specs=tok,
        out_shape=jax.ShapeDtypeStruct((NB, P, C), out_dtype),
        compiler_params=_cparams("parallel"),
        name="hy_inv_gate",
    )(y, ghi, glo, zb, u, bias_n.reshape(1, C))


def hyena_mixer(proj3, seq0, n_seq, col0, conv_w, conv_b, w1, b1, w2, b2, w3, freq, decay, bias):
    _, L, _ = proj3.shape
    width, out_dtype = HY_WIDTH, BF16
    P = min(L, HY_BLOCK)
    nb = L // P
    plan = HyPlan(P)
    u = hy_conv3(proj3, seq0, n_seq, col0, conv_w, conv_b, width).reshape(3, n_seq * nb, P, width)
    klin = hy_filters(L, w1, b1, w2, b2, w3, freq, decay, width)
    s = hy_fwd_dft(klin.reshape(HY_ORDER * 2 * nb, P, width), plan)
    z = u[0]
    for n in range(HY_ORDER):
        x = hy_fwd_dft(z, plan)
        y = hy_block_mul(x, s, plan, nb, n)
        z = hy_inv_gate(y, plan, z, u, 1 + n, bias[n], out_dtype if n == HY_ORDER - 1 else F32)
    return z.reshape(n_seq, L, width)


def _segments(B):
    assert B % CHAINS == 0 or CHAINS % B == 0
    return 1 if B % CHAINS == 0 else CHAINS // B


def kernel(x_prompt, x_sample, cache_na_k, cache_na_v, state_s5, state_rglru, c, c_ctx, norm_g, mod_w, mod_b, ffn_wg, ffn_wu, ffn_wd, even_w_in, even_w_out, s5_lam_re, s5_lam_im, s5_log_dt, s5_b_re, s5_b_im, s5_c_re, s5_c_im, s5_d, s5_glu_w, s5_glu_b, hy_conv_w, hy_conv_b, hy_w1, hy_b1, hy_w2, hy_b2, hy_w3, hy_freq, hy_decay, hy_bias, odd_w_in, odd_w_out, rg_conv_w, rg_conv_b, rg_wa, rg_ba, rg_wi, rg_bi, rg_lam, na_qn, na_kn, na_rpb):
    bp, lp, _ = x_prompt.shape
    bs, ls, _ = x_sample.shape
    n_p = bp * lp
    n_s = bs * ls
    n_tok = n_p + n_s
    assert n_p % ls == 0
    groups = ((0, n_p, bp, lp), (n_p, n_s, bs, ls))

    cond = jnp.concatenate([c_ctx[None], c, jnp.zeros((COND_ROWS - 1 - bs, D_MODEL), F32)], axis=0)
    m_all = adaln_all(cond, mod_w, mod_b)

    wg = ffn_wg.astype(BF16)
    wu = ffn_wu.astype(BF16)
    wd = ffn_wd.astype(BF16)
    w_in_e = even_w_in.astype(BF16)
    w_out_e = even_w_out.astype(BF16)
    w_in_o = odd_w_in.astype(BF16)
    w_out_o = odd_w_out.astype(BF16)

    x = jnp.concatenate([x_prompt.reshape(n_p, D_MODEL), x_sample.reshape(n_s, D_MODEL)], axis=0)
    s5_new, rg_new, k_new, v_new = [], [], [], []
    for li in range(DEPTH):
        x, h = ffn_block(x, m_all, norm_g, wg, wu, wd, li, 0, n_p, ls)
        if li % 2 == 0:
            ei = li // 2
            proj = in_proj(h, w_in_e, ei)
            prm = s5_params(s5_lam_re[ei], s5_lam_im[ei], s5_log_dt[ei], s5_b_re[ei], s5_b_im[ei],
                            s5_c_re[ei], s5_c_im[ei])
            ya, yb = [], []
            for gi, (t0, nt, B, L) in enumerate(groups):
                S = _segments(B)
                nc, Ls = B * S, L // S
                u3 = proj.reshape(n_tok // Ls, Ls, -1)
                if gi == 0:
                    y, hfin = s5_scan(u3, t0 // Ls, nc, *prm, jnp.zeros((2, nc, 2 * S5_STATE), F32))
                    fin = hfin.reshape(2, B, 2, S5_GROUPS, S5_P)
                    s5_new.append(jnp.transpose(fin, (1, 0, 3, 4, 2)))
                else:
                    h0 = jnp.transpose(state_s5[:, ei], (1, 0, 4, 2, 3)).reshape(2, B, 2 * S5_STATE)
                    y = s5_segmented(u3, t0 // Ls, prm, h0, B, S)
                ya.append(s5_glu(proj, t0, y.reshape(2, nt, S5_WIDTH), s5_d[ei].reshape(-1), s5_glu_w[ei], s5_glu_b[ei]))
                yb.append(hyena_mixer(proj.reshape(n_tok // L, L, -1), t0 // L, B, 1, hy_conv_w[ei], hy_conv_b[ei],
                                      hy_w1[ei], hy_b1[ei], hy_w2[ei], hy_b2[ei], hy_w3[ei], hy_freq[ei],
                                      hy_decay[ei], hy_bias[ei]).reshape(nt, HY_WIDTH))
            x = out_proj_residual(jnp.concatenate(ya, axis=0), jnp.concatenate(yb, axis=0), w_out_e, ei, x, m_all,
                                  li, n_p, ls)
        else:
            oi = li // 2
            proj = in_proj(h, w_in_o, oi)
            gprm = rglru_params(rg_wa[oi], rg_ba[oi], rg_wi[oi], rg_bi[oi])
            q, k, v, k_f32 = qkv_prep(proj, 2, na_qn[oi], na_kn[oi])
            yc, yd = [], []
            for gi, (t0, nt, B, L) in enumerate(groups):
                if gi == 0:
                    h0 = jnp.zeros((2, B, RG_WIDTH), F32)
                else:
                    h0 = jnp.transpose(state_rglru[:, oi], (1, 0, 2))
                y_c, hfin = rglru_mixer(proj, t0, rg_conv_w[oi], rg_conv_b[oi], gprm, rg_lam[oi], h0, B, L,
                                        _segments(B))
                yc.append(y_c)
                if gi == 0:
                    rg_new.append(jnp.transpose(hfin, (1, 0, 2)))
                    yd.append(ctx_attention(q, k, v, B, L))
                    k_new.append(k_f32[:nt].reshape(B, L, NA_HEADS, NA_HD))
                    v_new.append(proj[:nt, 4 * NA_WIDTH:].reshape(B, L, NA_HEADS, NA_HD))
                else:
                    past = cache_na_k.shape[2]
                    kc = cache_na_k[:, oi].reshape(B, past, NA_WIDTH).astype(BF16)
                    vc = cache_na_v[:, oi].reshape(B, past, NA_WIDTH).astype(BF16)
                    yd.append(na_attention(q, k, v, kc, vc, na_rpb[oi], t0, B, L))
            x = out_proj_residual(jnp.concatenate(yc, axis=0), jnp.concatenate(yd, axis=0), w_out_o, oi, x, m_all,
                                  li, n_p, ls)
        (x,) = ffn_block(x, m_all, norm_g, wg, wu, wd, li, 1, n_p, ls)

    y_prompt = x[:n_p].reshape(bp, lp, D_MODEL)
    y_sample = x[n_p:].reshape(bs, ls, D_MODEL)
    new_na_k = jnp.stack(k_new, axis=1)
    new_na_v = jnp.stack(v_new, axis=1)
    new_s5 = jnp.stack(s5_new, axis=1)
    new_rglru = jnp.stack(rg_new, axis=1)
    return (y_prompt, y_sample, new_na_k, new_na_v, new_s5, new_rglru)
```

```python
import functools
import math

import numpy as np
import jax
import jax.numpy as jnp
from jax import lax
from jax.experimental import pallas as pl
from jax.experimental.pallas import tpu as pltpu

F32 = jnp.float32
BF16 = jnp.bfloat16

D_MODEL = 2048
DEPTH = 4
N_MOD = 9
D_FF = 5632
EPS = 1e-6
MIX_HALF = D_MODEL // 2
GRID_W = 64
S5_WIDTH = MIX_HALF
S5_H = 16
S5_GROUPS = S5_WIDTH // S5_H
S5_P = 64
S5_STATE = S5_GROUPS * S5_P
S5_LAM_RE_MAX = -1e-4
HY_WIDTH = MIX_HALF
HY_ORDER = 2
HY_SHORT = 3
HY_BANDS = 16
HY_EMB = 1 + 2 * HY_BANDS
HY_EMB_PAD = 40
HY_FFN = 64
RG_WIDTH = MIX_HALF
RG_BLOCKS = 16
RG_BW = RG_WIDTH // RG_BLOCKS
RG_CONV = 4
RG_LEFT = RG_CONV // 2
RG_C = 8.0
NA_HD = 64
NA_HEADS = MIX_HALF // NA_HD
NA_WIDTH = NA_HEADS * NA_HD
NA_WIN_R = 8
NA_WIN_C = 16

VMEM_LIMIT_BYTES = 56 * 1024 * 1024
LANES = 128
CHAINS = 8
COND_ROWS = 8
TOKEN_TILE = 512
FF_TILE = 512
SCAN_T = 64
S5_COLS = 512
S5_BBLK = 256
S5_CBLK = 128
RG_GBLK = 256
HEAD_CHUNK = 256
NEG_BIG = -1e30
NA_QROWS = 4
NA_KROWS = NA_QROWS + NA_WIN_R
HY_BLOCK = 512
HY_CONV_TILE_ELEMS = 1024 * 1024
HY_MUL_LANES = 1024
HY_MUL_ACC_VREGS = 5


def _cparams(*sem):
    return pltpu.CompilerParams(dimension_semantics=sem, vmem_limit_bytes=VMEM_LIMIT_BYTES)


def _split_bf16(x):
    hi = x.astype(BF16)
    lo = (x - hi.astype(F32)).astype(BF16)
    return hi, lo


def _dot3(a_hi, a_lo, x):
    x_hi, x_lo = _split_bf16(x)
    return (jnp.dot(a_hi, x_hi, preferred_element_type=F32) + jnp.dot(a_lo, x_hi, preferred_element_type=F32)
            + jnp.dot(a_hi, x_lo, preferred_element_type=F32))


def _dot3_r(x, b_hi, b_lo):
    x_hi, x_lo = _split_bf16(x)
    return (jnp.dot(x_hi, b_hi, preferred_element_type=F32) + jnp.dot(x_hi, b_lo, preferred_element_type=F32)
            + jnp.dot(x_lo, b_hi, preferred_element_type=F32))


def _const_split(m):
    return _split_bf16(jnp.asarray(np.asarray(m, np.float32)))


def _gelu_tanh(x):
    return 0.5 * x * (1.0 + jnp.tanh(math.sqrt(2.0 / math.pi) * (x + 0.044715 * (x * x * x))))


def _adaln_kernel(cond_ref, w_ref, b_ref, o_ref):
    c = cond_ref[...]
    s = (c * jax.nn.sigmoid(c)).astype(BF16)
    o_ref[...] = jnp.dot(s, w_ref[...].astype(BF16), preferred_element_type=F32) + b_ref[...]


def adaln_all(cond, mod_w, mod_b):
    tn = 1024
    n_out = N_MOD * D_MODEL
    out = pl.pallas_call(
        _adaln_kernel,
        grid=(DEPTH, n_out // tn),
        in_specs=[
            pl.BlockSpec((COND_ROWS, D_MODEL), lambda l, j: (0, 0)),
            pl.BlockSpec((None, D_MODEL, tn), lambda l, j: (l, 0, j)),
            pl.BlockSpec((None, 1, tn), lambda l, j: (l, 0, j)),
        ],
        out_specs=pl.BlockSpec((None, COND_ROWS, tn), lambda l, j: (l, 0, j)),
        out_shape=jax.ShapeDtypeStruct((DEPTH, COND_ROWS, n_out), F32),
        compiler_params=_cparams("parallel", "parallel"),
        name="adaln",
    )(cond, mod_w, mod_b.reshape(DEPTH, 1, n_out))
    return out.reshape(DEPTH, COND_ROWS, N_MOD, D_MODEL)


def _rms_modulate(x, g, shift, scale):
    ms = jnp.mean(x * x, axis=-1, keepdims=True)
    return x * lax.rsqrt(ms + EPS) * g * (1.0 + scale) + shift


def _ffn_kernel(x_ref, m_ref, g_ref, wg_ref, wu_ref, wd_ref, o_ref, *rest,
                n_ff, in_rows, gate_row, g_row, next_rows, next_g_row):
    hn_ref = rest[0] if next_rows is not None else None
    h_s, acc_s = rest[-2:]
    k = pl.program_id(1)

    @pl.when(k == 0)
    def _():
        h = _rms_modulate(x_ref[...], g_ref[g_row:g_row + 1, :],
                          m_ref[in_rows[0]:in_rows[0] + 1, :], m_ref[in_rows[1]:in_rows[1] + 1, :])
        h_s[...] = h.astype(BF16)
        acc_s[...] = jnp.zeros_like(acc_s)

    h = h_s[...]
    gt = jnp.dot(h, wg_ref[...], preferred_element_type=F32)
    ut = jnp.dot(h, wu_ref[...], preferred_element_type=F32)
    a = (gt * jax.nn.sigmoid(gt) * ut).astype(BF16)
    acc_s[...] += jnp.dot(a, wd_ref[...], preferred_element_type=F32)

    @pl.when(k == n_ff - 1)
    def _():
        xn = x_ref[...] + 0.5 * m_ref[gate_row:gate_row + 1, :] * acc_s[...]
        o_ref[...] = xn
        if hn_ref is not None:
            hn = _rms_modulate(xn, g_ref[next_g_row:next_g_row + 1, :],
                               m_ref[next_rows[0]:next_rows[0] + 1, :], m_ref[next_rows[1]:next_rows[1] + 1, :])
            hn_ref[...] = hn.astype(BF16)


def _cond_row(i, tm, n_prompt_tok, sample_len):
    return jnp.where(i * tm < n_prompt_tok, 0, 1 + (i * tm - n_prompt_tok) // sample_len)


def ffn_block(x, m_all, norm_g, wg, wu, wd, li, half, n_prompt_tok, sample_len):
    n_tok = x.shape[0]
    tm, tf = TOKEN_TILE, FF_TILE
    n_ff = D_FF // tf
    if half == 0:
        cfg = dict(in_rows=(0, 1), gate_row=2, g_row=0, next_rows=(3, 4), next_g_row=1)
    else:
        cfg = dict(in_rows=(6, 7), gate_row=8, g_row=2, next_rows=None, next_g_row=None)
    row = functools.partial(_cond_row, tm=tm, n_prompt_tok=n_prompt_tok, sample_len=sample_len)
    tok = pl.BlockSpec((tm, D_MODEL), lambda i, k: (i, 0))
    out_specs = [tok]
    out_shape = [jax.ShapeDtypeStruct((n_tok, D_MODEL), F32)]
    if half == 0:
        out_specs.append(tok)
        out_shape.append(jax.ShapeDtypeStruct((n_tok, D_MODEL), BF16))
    return pl.pallas_call(
        functools.partial(_ffn_kernel, n_ff=n_ff, **cfg),
        grid=(n_tok // tm, n_ff),
        in_specs=[
            tok,
            pl.BlockSpec((None, None, N_MOD, D_MODEL), lambda i, k: (li, row(i), 0, 0)),
            pl.BlockSpec((None, 3, D_MODEL), lambda i, k: (li, 0, 0)),
            pl.BlockSpec((None, None, D_MODEL, tf), lambda i, k: (li, half, 0, k)),
            pl.BlockSpec((None, None, D_MODEL, tf), lambda i, k: (li, half, 0, k)),
            pl.BlockSpec((None, None, tf, D_MODEL), lambda i, k: (li, half, k, 0)),
        ],
        out_specs=out_specs,
        out_shape=out_shape,
        scratch_shapes=[pltpu.VMEM((tm, D_MODEL), BF16), pltpu.VMEM((tm, D_MODEL), F32)],
        compiler_params=_cparams("parallel", "arbitrary"),
        name=f"ffn_l{li}_h{half}",
    )(x, m_all, norm_g, wg, wu, wd)


def _proj_kernel(a_ref, w_ref, o_ref):
    o_ref[...] = jnp.dot(a_ref[...], w_ref[...], preferred_element_type=F32).astype(o_ref.dtype)


def in_proj(h, w, idx):
    n_tok = h.shape[0]
    n_out = w.shape[-1]
    tm, tn = TOKEN_TILE, n_out // 2
    return pl.pallas_call(
        _proj_kernel,
        grid=(n_out // tn, n_tok // tm),
        in_specs=[pl.BlockSpec((tm, D_MODEL), lambda j, i: (i, 0)),
                  pl.BlockSpec((None, D_MODEL, tn), lambda j, i: (idx, 0, j))],
        out_specs=pl.BlockSpec((tm, tn), lambda j, i: (i, j)),
        out_shape=jax.ShapeDtypeStruct((n_tok, n_out), F32),
        compiler_params=_cparams("parallel", "parallel"),
        name="in_proj",
    )(h, w)


def _out_proj_kernel(ya_ref, yb_ref, wa_ref, wb_ref, x_ref, m_ref, o_ref):
    y = (jnp.dot(ya_ref[...], wa_ref[...], preferred_element_type=F32)
         + jnp.dot(yb_ref[...], wb_ref[...], preferred_element_type=F32))
    o_ref[...] = x_ref[...] + m_ref[5:6, :] * y


def out_proj_residual(ya, yb, w, idx, x, m_all, li, n_prompt_tok, sample_len):
    n_tok = x.shape[0]
    tm = TOKEN_TILE
    row = functools.partial(_cond_row, tm=tm, n_prompt_tok=n_prompt_tok, sample_len=sample_len)
    return pl.pallas_call(
        _out_proj_kernel,
        grid=(n_tok // tm,),
        in_specs=[pl.BlockSpec((tm, MIX_HALF), lambda i: (i, 0)),
                  pl.BlockSpec((tm, MIX_HALF), lambda i: (i, 0)),
                  pl.BlockSpec((None, MIX_HALF, D_MODEL), lambda i: (idx, 0, 0)),
                  pl.BlockSpec((None, MIX_HALF, D_MODEL), lambda i: (idx, 1, 0)),
                  pl.BlockSpec((tm, D_MODEL), lambda i: (i, 0)),
                  pl.BlockSpec((None, None, N_MOD, D_MODEL), lambda i: (li, row(i), 0, 0))],
        out_specs=pl.BlockSpec((tm, D_MODEL), lambda i: (i, 0)),
        out_shape=jax.ShapeDtypeStruct((n_tok, D_MODEL), F32),
        compiler_params=_cparams("parallel"),
        name="out_proj",
    )(ya, yb, w, w, x, m_all)


def _s5_scan_kernel(u_ref, bw_ref, a_ref, cw_ref, h0_ref, *rest, n_t, t_tile, with_y):
    if with_y:
        y_ref, hfin_ref, br_s, bi_s, hr_s, hi_s = rest
    else:
        hfin_ref, br_s, bi_s, hr_s, hi_s = rest
    d = pl.program_id(0)
    jt = pl.program_id(2)
    rows = t_tile * CHAINS

    @pl.when(jt == 0)
    def _():
        hr_s[...] = h0_ref[:, :S5_STATE]
        hi_s[...] = h0_ref[:, S5_STATE:]

    u = jnp.swapaxes(u_ref[...], 0, 1).reshape(rows, S5_WIDTH).astype(BF16)
    n_bblk = S5_WIDTH // S5_BBLK
    ncol = S5_STATE // n_bblk
    for blk in range(n_bblk):
        bu = jnp.dot(u[:, blk * S5_BBLK:(blk + 1) * S5_BBLK], bw_ref[blk], preferred_element_type=F32)
        br_s[:, blk * ncol:(blk + 1) * ncol] = bu[:, :ncol]
        bi_s[:, blk * ncol:(blk + 1) * ncol] = bu[:, ncol:]

    for cc in range(S5_STATE // S5_COLS):
        cs = slice(cc * S5_COLS, (cc + 1) * S5_COLS)
        ar = a_ref[0, :, cs]
        ai = a_ref[1, :, cs]

        def step(i, carry):
            hr, hi = carry
            t = jnp.where(d == 0, i, t_tile - 1 - i)
            r0 = pl.multiple_of(t * CHAINS, CHAINS)
            nhr = ar * hr - ai * hi + br_s[pl.ds(r0, CHAINS), cs]
            nhi = ar * hi + ai * hr + bi_s[pl.ds(r0, CHAINS), cs]
            if with_y:
                br_s[pl.ds(r0, CHAINS), cs] = nhr
                bi_s[pl.ds(r0, CHAINS), cs] = nhi
            return nhr, nhi

        hr, hi = lax.fori_loop(0, t_tile, step, (hr_s[:, cs], hi_s[:, cs]), unroll=4)
        hr_s[:, cs] = hr
        hi_s[:, cs] = hi

    if with_y:
        n_cblk = S5_WIDTH // S5_CBLK
        kc = S5_STATE // n_cblk
        for i in range(n_cblk):
            yi = jnp.dot(br_s[:, i * kc:(i + 1) * kc].astype(BF16), cw_ref[0, i], preferred_element_type=F32)
            yi = yi + jnp.dot(bi_s[:, i * kc:(i + 1) * kc].astype(BF16), cw_ref[1, i], preferred_element_type=F32)
            y_ref[:, :, i * S5_CBLK:(i + 1) * S5_CBLK] = jnp.swapaxes(yi.reshape(t_tile, CHAINS, S5_CBLK), 0, 1)

    @pl.when(jt == n_t - 1)
    def _():
        hfin_ref[:, :S5_STATE] = hr_s[...]
        hfin_ref[:, S5_STATE:] = hi_s[...]


def s5_scan(u3, seq0, nc, bw, a_b, cw, h0, with_y=True):
    _, L, _ = u3.shape
    assert seq0 % CHAINS == 0 and nc % CHAINS == 0
    c0 = seq0 // CHAINS
    t_tile = min(SCAN_T, L)
    n_t = L // t_tile
    tix = lambda d, j: jnp.where(d == 0, j, n_t - 1 - j)
    fin_spec = pl.BlockSpec((None, CHAINS, 2 * S5_STATE), lambda d, c, j: (d, c, 0))
    fin_shape = jax.ShapeDtypeStruct((2, nc, 2 * S5_STATE), F32)
    out_specs, out_shape = [fin_spec], [fin_shape]
    if with_y:
        out_specs = [pl.BlockSpec((None, CHAINS, t_tile, S5_WIDTH), lambda d, c, j: (d, c, tix(d, j), 0))] + out_specs
        out_shape = [jax.ShapeDtypeStruct((2, nc, L, S5_WIDTH), F32)] + out_shape
    return pl.pallas_call(
        functools.partial(_s5_scan_kernel, n_t=n_t, t_tile=t_tile, with_y=with_y),
        grid=(2, nc // CHAINS, n_t),
        in_specs=[
            pl.BlockSpec((CHAINS, t_tile, S5_WIDTH), lambda d, c, j: (c0 + c, tix(d, j), 0)),
            pl.BlockSpec((None,) + bw.shape[1:], lambda d, c, j: (d, 0, 0, 0)),
            pl.BlockSpec((None,) + a_b.shape[1:], lambda d, c, j: (d, 0, 0, 0)),
            pl.BlockSpec((None,) + cw.shape[1:], lambda d, c, j: (d, 0, 0, 0, 0)),
            fin_spec,
        ],
        out_specs=out_specs,
        out_shape=out_shape,
        scratch_shapes=[pltpu.VMEM((t_tile * CHAINS, S5_STATE), F32), pltpu.VMEM((t_tile * CHAINS, S5_STATE), F32),
                        pltpu.VMEM((CHAINS, S5_STATE), F32), pltpu.VMEM((CHAINS, S5_STATE), F32)],
        compiler_params=_cparams("parallel", "parallel", "arbitrary"),
        name="s5_scan" if with_y else "s5_state",
    )(u3, bw, a_b, cw, h0)


def _seg_init_kernel(hl_ref, h0_ref, ar_ref, ai_ref, o_ref, *, n_seg, width, cplx):
    cr = h0_ref[:, :width]
    ci = h0_ref[:, width:] if cplx else None
    for k in range(n_seg):
        o_ref[k, :, :width] = cr
        ar = ar_ref[k]
        if cplx:
            o_ref[k, :, width:] = ci
            ai = ai_ref[k]
            cr, ci = (ar * cr - ai * ci + hl_ref[k, :, :width], ar * ci + ai * cr + hl_ref[k, :, width:])
        else:
            cr = ar * cr + hl_ref[k]


def seg_init(hl, h0, ar, ai=None):
    _, S, B, W2 = hl.shape
    cplx = ai is not None
    width = W2 // 2 if cplx else W2
    if ai is None:
        ai = ar
    aspec = pl.BlockSpec((None,) + ar.shape[1:], lambda d: (d, 0, 0, 0))
    return pl.pallas_call(
        functools.partial(_seg_init_kernel, n_seg=S, width=width, cplx=cplx),
        grid=(2,),
        in_specs=[pl.BlockSpec((None, S, B, W2), lambda d: (d, 0, 0, 0)),
                  pl.BlockSpec((None, B, W2), lambda d: (d, 0, 0)), aspec, aspec],
        out_specs=pl.BlockSpec((None, S, B, W2), lambda d: (d, 0, 0, 0)),
        out_shape=jax.ShapeDtypeStruct((2, S, B, W2), F32),
        compiler_params=_cparams("parallel"),
        name="seg_init",
    )(hl, h0, ar, ai)


def _to_proc_order(x, B, S):
    x = jnp.transpose(x.reshape(2, B, S, -1), (0, 2, 1, 3))
    return jnp.stack([x[0], x[1, ::-1]], axis=0)


def _from_proc_order(x, B, S):
    x = jnp.stack([x[0], x[1, ::-1]], axis=0)
    return jnp.transpose(x, (0, 2, 1, 3)).reshape(2, B * S, -1)


def _cpow(ar, ai, n):
    rr, ri = None, None
    br, bi = ar, ai
    while n:
        if n & 1:
            rr, ri = (br, bi) if rr is None else (rr * br - ri * bi, rr * bi + ri * br)
        n >>= 1
        if n:
            br, bi = br * br - bi * bi, 2.0 * br * bi
    return rr, ri


def s5_segmented(u3, seq0, prm, h0, B, S):
    bw, a_b, cw = prm
    Ls = u3.shape[1]
    zeros = jnp.zeros((2, B * S, 2 * S5_STATE), F32)
    (hloc,) = s5_scan(u3, seq0, B * S, bw, a_b, cw, zeros, with_y=False)
    pr, pi = _cpow(a_b[:, 0, 0], a_b[:, 1, 0], Ls)
    rep = lambda p: jnp.broadcast_to(p[:, None, None, :], (2, S, 1, S5_STATE))
    init = seg_init(_to_proc_order(hloc, B, S), h0, rep(pr), rep(pi))
    y, _ = s5_scan(u3, seq0, B * S, bw, a_b, cw, _from_proc_order(init, B, S), with_y=True)
    return y


def s5_params(lam_re, lam_im, log_dt, b_re, b_im, c_re, c_im):
    lr = jnp.minimum(lam_re, S5_LAM_RE_MAX)
    li = lam_im
    dt = jnp.exp(log_dt)[..., None]
    mag = jnp.exp(lr * dt)
    abr, abi = mag * jnp.cos(li * dt), mag * jnp.sin(li * dt)
    den = lr * lr + li * li
    cr = ((abr - 1.0) * lr + abi * li) / den
    ci = (abi * lr - (abr - 1.0) * li) / den
    bbr = cr[..., None] * b_re - ci[..., None] * b_im
    bbi = cr[..., None] * b_im + ci[..., None] * b_re
    n_bblk = S5_WIDTH // S5_BBLK
    gb = S5_GROUPS // n_bblk
    bb = jnp.stack([bbr, bbi], axis=1).reshape(2, 2, n_bblk, gb, S5_P, S5_H)
    bw = jnp.einsum('drbgph,gk->dbghrkp', bb, jnp.eye(gb, dtype=F32)).reshape(2, n_bblk, gb * S5_H, 2 * gb * S5_P)
    n_cblk = S5_WIDTH // S5_CBLK
    gc = S5_GROUPS // n_cblk
    cc = jnp.stack([c_re, -c_im], axis=1).reshape(2, 2, n_cblk, gc, S5_H, S5_P)
    cw = jnp.einsum('drbghp,gk->drbgpkh', cc, jnp.eye(gc, dtype=F32)).reshape(2, 2, n_cblk, gc * S5_P, gc * S5_H)
    a_b = jnp.stack([abr.reshape(2, S5_STATE), abi.reshape(2, S5_STATE)], axis=1)
    a_b = jnp.broadcast_to(a_b[:, :, None, :], (2, 2, CHAINS, S5_STATE))
    return bw.astype(BF16), a_b, cw.astype(BF16)


def _s5_glu_kernel(u_ref, y_ref, dsk_ref, w_ref, b_ref, o_ref):
    y = dsk_ref[...] * u_ref[...] + y_ref[0] + y_ref[1]
    z = _gelu_tanh(y)
    gate = jnp.dot(z.astype(BF16), w_ref[...], preferred_element_type=F32) + b_ref[...]
    o_ref[...] = (z * jax.nn.sigmoid(gate)).astype(o_ref.dtype)


def s5_glu(proj, tok0, y2d, d_skip, glu_w, glu_b):
    R = y2d.shape[1]
    tm = TOKEN_TILE
    r0 = tok0 // tm
    return pl.pallas_call(
        _s5_glu_kernel,
        grid=(R // tm,),
        in_specs=[pl.BlockSpec((tm, S5_WIDTH), lambda i: (r0 + i, 0)),
                  pl.BlockSpec((2, tm, S5_WIDTH), lambda i: (0, i, 0)),
                  pl.BlockSpec((1, S5_WIDTH), lambda i: (0, 0)),
                  pl.BlockSpec((S5_WIDTH, S5_WIDTH), lambda i: (0, 0)),
                  pl.BlockSpec((1, S5_WIDTH), lambda i: (0, 0))],
        out_specs=pl.BlockSpec((tm, S5_WIDTH), lambda i: (i, 0)),
        out_shape=jax.ShapeDtypeStruct((R, S5_WIDTH), BF16),
        compiler_params=_cparams("parallel"),
        name="s5_glu",
    )(proj, y2d, d_skip.reshape(1, S5_WIDTH), glu_w.astype(BF16), glu_b.reshape(1, S5_WIDTH))


def _rglru_kernel(xp_ref, xc_ref, xn_ref, cw_ref, cb_ref, wa_ref, ba_ref, wi_ref, bi_ref, lam_ref, h0_ref, *rest,
                  n_t, t_tile, seg, n_seg):
    if seg:
        h_ref, p_ref, hfin_ref, pfin_ref, a_s, b_s, hc_s, ht_s, pc_s, pt_s = rest
    else:
        h_ref, hfin_ref, a_s, b_s, hc_s, ht_s = rest
    d = pl.program_id(0)
    jt = pl.program_id(2)
    tj = jnp.where(d == 0, jt, n_t - 1 - jt)
    rows = t_tile * CHAINS

    @pl.when(jt == 0)
    def _():
        hc_s[...] = h0_ref[...]
        if seg:
            pc_s[...] = jnp.ones_like(pc_s)

    prev_in = jnp.swapaxes(xp_ref[...], 0, 1)[CHAINS - RG_LEFT:]
    next_in = jnp.swapaxes(xn_ref[...], 0, 1)[:RG_CONV - 1 - RG_LEFT]
    if seg:
        cidx = lax.broadcasted_iota(jnp.int32, (1, CHAINS, 1), 1) % n_seg
        first_t = tj == 0
        last_t = tj == n_t - 1
        prev_nb = pltpu.roll(prev_in, 1, axis=1)
        next_nb = pltpu.roll(next_in, CHAINS - 1, axis=1)
        prev = jnp.where(first_t, jnp.where(cidx == 0, 0.0, prev_nb), prev_in)
        nxt = jnp.where(last_t, jnp.where(cidx == n_seg - 1, 0.0, next_nb), next_in)
    else:
        prev = prev_in * (tj > 0).astype(F32)
        nxt = next_in * (tj < n_t - 1).astype(F32)
    xcat = jnp.concatenate([prev, jnp.swapaxes(xc_ref[...], 0, 1), nxt], axis=0)
    xc = cb_ref[...] + cw_ref[0:1, :] * xcat[0:t_tile]
    for j in range(1, RG_CONV):
        xc = xc + cw_ref[j:j + 1, :] * xcat[j:j + t_tile]
    x2 = xc.reshape(rows, RG_WIDTH)
    xb = x2.astype(BF16)

    ra, ia = [], []
    for j in range(RG_WIDTH // RG_GBLK):
        xs = xb[:, j * RG_GBLK:(j + 1) * RG_GBLK]
        ra.append(jnp.dot(xs, wa_ref[j], preferred_element_type=F32))
        ia.append(jnp.dot(xs, wi_ref[j], preferred_element_type=F32))
    r = jax.nn.sigmoid(jnp.concatenate(ra, axis=1) + ba_ref[...])
    ig = jax.nn.sigmoid(jnp.concatenate(ia, axis=1) + bi_ref[...])
    nl = -lam_ref[...]
    sp = jnp.maximum(nl, 0.0) + jnp.log1p(jnp.exp(-jnp.abs(nl)))
    log_a = (-RG_C) * r * sp
    a_s[...] = jnp.exp(log_a)
    th = jnp.tanh(log_a)
    one_minus_a2 = (-2.0) * th / (1.0 - th)
    b_s[...] = jnp.sqrt(one_minus_a2) * (ig * x2)

    def step(i, carry):
        t = jnp.where(d == 0, i, t_tile - 1 - i)
        r0 = pl.multiple_of(t * CHAINS, CHAINS)
        a = a_s[pl.ds(r0, CHAINS), :]
        if seg:
            h, p = carry
            h = a * h + b_s[pl.ds(r0, CHAINS), :]
            p = a * p
            ht_s[t] = h
            pt_s[t] = p
            return h, p
        h = a * carry + b_s[pl.ds(r0, CHAINS), :]
        ht_s[t] = h
        return h

    if seg:
        h, p = lax.fori_loop(0, t_tile, step, (hc_s[...], pc_s[...]), unroll=8)
        pc_s[...] = p
        p_ref[...] = jnp.swapaxes(pt_s[...], 0, 1)
    else:
        h = lax.fori_loop(0, t_tile, step, hc_s[...], unroll=8)
    hc_s[...] = h
    h_ref[...] = jnp.swapaxes(ht_s[...], 0, 1)

    @pl.when(jt == n_t - 1)
    def _():
        hfin_ref[...] = h
        if seg:
            pfin_ref[...] = p


def rglru_scan(x3, seq0, nc, conv_w, conv_b, wa, ba, wi, bi, lam, h0, n_seg=1):
    _, L, _ = x3.shape
    seg = n_seg > 1
    assert seq0 % CHAINS == 0 and nc % CHAINS == 0 and (not seg or nc == CHAINS)
    c0 = seq0 // CHAINS
    t_tile = min(SCAN_T, L)
    n_t = L // t_tile
    tix = lambda d, j: jnp.where(d == 0, j, n_t - 1 - j)
    per = t_tile // CHAINS
    n_hb = L // CHAINS
    if seg:
        prev_ix = lambda d, c, j: (c0 + c, (tix(d, j) * per - 1) % n_hb, 0)
        next_ix = lambda d, c, j: (c0 + c, ((tix(d, j) + 1) * per) % n_hb, 0)
    else:
        prev_ix = lambda d, c, j: (c0 + c, jnp.maximum(tix(d, j) * per - 1, 0), 0)
        next_ix = lambda d, c, j: (c0 + c, jnp.minimum((tix(d, j) + 1) * per, n_hb - 1), 0)
    hspec = pl.BlockSpec((None, CHAINS, t_tile, RG_WIDTH), lambda d, c, j: (d, c, tix(d, j), 0))
    fspec = pl.BlockSpec((None, CHAINS, RG_WIDTH), lambda d, c, j: (d, c, 0))
    hshape = jax.ShapeDtypeStruct((2, nc, L, RG_WIDTH), F32)
    fshape = jax.ShapeDtypeStruct((2, nc, RG_WIDTH), F32)
    tile = pltpu.VMEM((t_tile, CHAINS, RG_WIDTH), F32)
    scratch = [pltpu.VMEM((t_tile * CHAINS, RG_WIDTH), F32), pltpu.VMEM((t_tile * CHAINS, RG_WIDTH), F32),
               pltpu.VMEM((CHAINS, RG_WIDTH), F32), tile]
    if seg:
        out_specs, out_shape = [hspec, hspec, fspec, fspec], [hshape, hshape, fshape, fshape]
        scratch += [pltpu.VMEM((CHAINS, RG_WIDTH), F32), tile]
    else:
        out_specs, out_shape = [hspec, fspec], [hshape, fshape]
    halo = lambda ix: pl.BlockSpec((CHAINS, CHAINS, RG_WIDTH), ix)
    return pl.pallas_call(
        functools.partial(_rglru_kernel, n_t=n_t, t_tile=t_tile, seg=seg, n_seg=n_seg),
        grid=(2, nc // CHAINS, n_t),
        in_specs=[
            halo(prev_ix),
            pl.BlockSpec((CHAINS, t_tile, RG_WIDTH), lambda d, c, j: (c0 + c, tix(d, j), 0)),
            halo(next_ix),
            pl.BlockSpec((RG_CONV, RG_WIDTH), lambda d, c, j: (0, 0)),
            pl.BlockSpec((1, RG_WIDTH), lambda d, c, j: (0, 0)),
            pl.BlockSpec((None,) + wa.shape[1:], lambda d, c, j: (d, 0, 0, 0)),
            pl.BlockSpec((None, 1, RG_WIDTH), lambda d, c, j: (d, 0, 0)),
            pl.BlockSpec((None,) + wi.shape[1:], lambda d, c, j: (d, 0, 0, 0)),
            pl.BlockSpec((None, 1, RG_WIDTH), lambda d, c, j: (d, 0, 0)),
            pl.BlockSpec((None, 1, RG_WIDTH), lambda d, c, j: (d, 0, 0)),
            fspec,
        ],
        out_specs=out_specs,
        out_shape=out_shape,
        scratch_shapes=scratch,
        compiler_params=_cparams("parallel", "parallel", "arbitrary"),
        name="rglru_seg" if seg else "rglru_scan",
    )(x3, x3, x3, conv_w, conv_b.reshape(1, RG_WIDTH), wa, ba, wi, bi, lam.reshape(2, 1, RG_WIDTH), h0)


def rglru_params(w_a, b_a, w_i, b_i):
    nb = RG_WIDTH // RG_GBLK
    per = RG_GBLK // RG_BW
    eye = jnp.eye(per, dtype=F32)

    def bd(w):
        w = w.reshape(2, nb, per, RG_BW, RG_BW)
        return jnp.einsum('dbnce,nm->dbncme', w, eye).reshape(2, nb, RG_GBLK, RG_GBLK).astype(BF16)

    return bd(w_a), b_a.reshape(2, 1, RG_WIDTH), bd(w_i), b_i.reshape(2, 1, RG_WIDTH)


def _rg_combine_kernel(h_ref, g_ref, *rest, seg):
    if seg:
        p_ref, i_ref, o_ref = rest
        hsum = h_ref[0] + p_ref[0] * i_ref[0] + h_ref[1] + p_ref[1] * i_ref[1]
    else:
        (o_ref,) = rest
        hsum = h_ref[0] + h_ref[1]
    o_ref[...] = (hsum * _gelu_tanh(g_ref[...])).astype(o_ref.dtype)


def rg_combine(h2, proj, tok0, col_blk, p2=None, init=None, seg_len=None):
    R = h2.shape[1]
    seg = p2 is not None
    tm = math.gcd(TOKEN_TILE, seg_len) if seg else TOKEN_TILE
    r0 = tok0 // tm
    hspec = pl.BlockSpec((2, tm, RG_WIDTH), lambda i: (0, i, 0))
    in_specs = [hspec, pl.BlockSpec((tm, RG_WIDTH), lambda i: (r0 + i, col_blk))]
    args = [h2, proj]
    if seg:
        in_specs += [hspec, pl.BlockSpec((2, None, 1, RG_WIDTH), lambda i: (0, (i * tm) // seg_len, 0, 0))]
        args += [p2, init.reshape(2, -1, 1, RG_WIDTH)]
    return pl.pallas_call(
        functools.partial(_rg_combine_kernel, seg=seg),
        grid=(R // tm,),
        in_specs=in_specs,
        out_specs=pl.BlockSpec((tm, RG_WIDTH), lambda i: (i, 0)),
        out_shape=jax.ShapeDtypeStruct((R, RG_WIDTH), BF16),
        compiler_params=_cparams("parallel"),
        name="rg_combine",
    )(*args)


def rglru_mixer(proj, tok0, conv_w, conv_b, gprm, lam, h0, B, L, S):
    nc, Ls = B * S, L // S
    x3 = proj.reshape(proj.shape[0] // Ls, Ls, -1)
    seq0 = tok0 // Ls
    if S == 1:
        h, hfin = rglru_scan(x3, seq0, nc, conv_w, conv_b, *gprm, lam, h0)
        return rg_combine(h.reshape(2, nc * Ls, RG_WIDTH), proj, tok0, 1), hfin
    zeros = jnp.zeros((2, nc, RG_WIDTH), F32)
    h, p, hfin, pfin = rglru_scan(x3, seq0, nc, conv_w, conv_b, *gprm, lam, zeros, n_seg=S)
    init = seg_init(_to_proc_order(hfin, B, S), h0, _to_proc_order(pfin, B, S))
    init = _from_proc_order(init, B, S)
    y = rg_combine(h.reshape(2, nc * Ls, RG_WIDTH), proj, tok0, 1, p2=p.reshape(2, nc * Ls, RG_WIDTH), init=init,
                   seg_len=Ls)
    return y, hfin


def _head_rms(x, ones_bd, gain):
    sq = x * x
    hi, lo = _split_bf16(sq)
    parts = []
    for j in range(NA_WIDTH // HEAD_CHUNK):
        cs = slice(j * HEAD_CHUNK, (j + 1) * HEAD_CHUNK)
        parts.append(jnp.dot(hi[:, cs], ones_bd, preferred_element_type=F32)
                     + jnp.dot(lo[:, cs], ones_bd, preferred_element_type=F32))
    ms = jnp.concatenate(parts, axis=1) * (1.0 / NA_HD)
    return x * lax.rsqrt(ms + EPS) * gain


def _qkv_prep_kernel(q_ref, k_ref, v_ref, ones_ref, gq_ref, gk_ref, qo_ref, ko_ref, vo_ref, kf_ref):
    ones_bd = ones_ref[...]
    qo_ref[...] = _head_rms(q_ref[...], ones_bd, gq_ref[...]).astype(BF16)
    kn = _head_rms(k_ref[...], ones_bd, gk_ref[...])
    ko_ref[...] = kn.astype(BF16)
    kf_ref[...] = kn
    vo_ref[...] = v_ref[...].astype(BF16)


def qkv_prep(proj, col0, qn_g, kn_g):
    n_tok = proj.shape[0]
    tm = TOKEN_TILE
    ones_bd = jnp.kron(jnp.eye(HEAD_CHUNK // NA_HD, dtype=F32), jnp.ones((NA_HD, NA_HD), F32)).astype(BF16)
    gq = jnp.tile(qn_g, NA_HEADS).reshape(1, NA_WIDTH)
    gk = jnp.tile(kn_g, NA_HEADS).reshape(1, NA_WIDTH)
    blk = lambda c: pl.BlockSpec((tm, NA_WIDTH), lambda i: (i, c))
    full = lambda s: pl.BlockSpec(s, lambda i: (0, 0))
    return pl.pallas_call(
        _qkv_prep_kernel,
        grid=(n_tok // tm,),
        in_specs=[blk(col0), blk(col0 + 1), blk(col0 + 2), full((HEAD_CHUNK, HEAD_CHUNK)),
                  full((1, NA_WIDTH)), full((1, NA_WIDTH))],
        out_specs=[blk(0), blk(0), blk(0), blk(0)],
        out_shape=[jax.ShapeDtypeStruct((n_tok, NA_WIDTH), BF16)] * 3 + [jax.ShapeDtypeStruct((n_tok, NA_WIDTH), F32)],
        compiler_params=_cparams("parallel"),
        name="qkv_prep",
    )(proj, proj, proj, ones_bd, gq, gk)


_NT = (((1,), (1,)), ((), ()))


def _ctx_attn_kernel(q_ref, k_ref, v_ref, o_ref):
    scale = NA_HD ** -0.5
    outs = []
    for h in range(NA_HEADS):
        hs = slice(h * NA_HD, (h + 1) * NA_HD)
        s = lax.dot_general(q_ref[:, hs], k_ref[:, hs], _NT, preferred_element_type=F32) * scale
        m = jnp.max(s, axis=-1, keepdims=True)
        e = jnp.exp(s - m)
        l = jnp.sum(e, axis=-1, keepdims=True)
        o = jnp.dot(e.astype(BF16), v_ref[:, hs], preferred_element_type=F32)
        outs.append(o / l)
        if h % 2 == 1:
            o_ref[:, (h - 1) * NA_HD:(h + 1) * NA_HD] = jnp.concatenate(outs, axis=1).astype(o_ref.dtype)
            outs = []


def ctx_attention(q, k, v, n_seq, seq_len):
    blk = pl.BlockSpec((seq_len, NA_WIDTH), lambda b: (b, 0))
    return pl.pallas_call(
        _ctx_attn_kernel,
        grid=(n_seq,),
        in_specs=[blk, blk, blk],
        out_specs=blk,
        out_shape=jax.ShapeDtypeStruct((n_seq * seq_len, NA_WIDTH), BF16),
        compiler_params=_cparams("parallel"),
        name="ctx_attn",
    )(q, k, v)


def _na_attn_kernel(q_ref, k_ref, v_ref, kc_ref, vc_ref, bias_ref, o_ref, *, rows):
    rb = pl.program_id(2)
    ws = jnp.clip(rb * NA_QROWS - NA_WIN_R // 2, 0, rows - NA_KROWS)
    k0 = pl.multiple_of(ws * GRID_W, GRID_W)
    nloc = NA_KROWS * GRID_W
    scale = NA_HD ** -0.5
    outs = []
    for h in range(LANES // NA_HD):
        hs = slice(h * NA_HD, (h + 1) * NA_HD)
        qh = q_ref[:, hs]
        s_loc = lax.dot_general(qh, k_ref[pl.ds(k0, nloc), hs], _NT, preferred_element_type=F32) * scale + bias_ref[h]
        s_ctx = lax.dot_general(qh, kc_ref[:, hs], _NT, preferred_element_type=F32) * scale
        m = jnp.maximum(jnp.max(s_loc, axis=-1, keepdims=True), jnp.max(s_ctx, axis=-1, keepdims=True))
        e_loc = jnp.exp(s_loc - m)
        e_ctx = jnp.exp(s_ctx - m)
        l = jnp.sum(e_loc, axis=-1, keepdims=True) + jnp.sum(e_ctx, axis=-1, keepdims=True)
        o = (jnp.dot(e_loc.astype(BF16), v_ref[pl.ds(k0, nloc), hs], preferred_element_type=F32)
             + jnp.dot(e_ctx.astype(BF16), vc_ref[:, hs], preferred_element_type=F32))
        outs.append(o / l)
    o_ref[...] = jnp.concatenate(outs, axis=1).astype(o_ref.dtype)


def na_bias_table(rpb, rows):
    qc = np.arange(GRID_W)
    kc = np.arange(GRID_W)
    cs = np.clip(qc - NA_WIN_C // 2, 0, GRID_W - NA_WIN_C)
    cvalid = (kc[None, :] >= cs[:, None]) & (kc[None, :] < cs[:, None] + NA_WIN_C)
    colrel = np.clip(kc[None, :] - qc[:, None] + NA_WIN_C - 1, 0, 2 * NA_WIN_C - 2)
    toep = rpb[:, :, colrel]
    rowrel = np.zeros((3, NA_QROWS, NA_KROWS), np.int32)
    rvalid = np.zeros((3, NA_QROWS, NA_KROWS), bool)
    for cls, r0 in enumerate((0, NA_WIN_R // 2, rows - NA_QROWS)):
        ws = min(max(r0 - NA_WIN_R // 2, 0), rows - NA_KROWS)
        for qr in range(NA_QROWS):
            r = r0 + qr
            rs = min(max(r - NA_WIN_R // 2, 0), rows - NA_WIN_R)
            for j in range(NA_KROWS):
                kr = ws + j
                rvalid[cls, qr, j] = rs <= kr < rs + NA_WIN_R
                rowrel[cls, qr, j] = min(max(kr - r + NA_WIN_R - 1, 0), 2 * NA_WIN_R - 2)
    tab = toep[:, rowrel]
    valid = rvalid[:, :, :, None, None] & cvalid[None, None, None]
    tab = jnp.where(valid[None], tab, NEG_BIG)
    tab = jnp.transpose(tab, (0, 1, 2, 4, 3, 5))
    return tab.reshape(rpb.shape[0], 3, NA_QROWS * GRID_W, NA_KROWS * GRID_W)


def na_attention(q, k, v, k_ctx, v_ctx, rpb, tok0, n_seq, seq_len):
    rows = seq_len // GRID_W
    past = k_ctx.shape[1]
    qtok = NA_QROWS * GRID_W
    n_rb = rows // NA_QROWS
    assert tok0 % seq_len == 0 and rows % NA_QROWS == 0 and rows >= NA_KROWS
    seq0 = tok0 // seq_len
    qb0 = tok0 // qtok
    hp = LANES // NA_HD
    bias_tab = na_bias_table(rpb, rows)
    cls = lambda rb: jnp.where(rb == 0, 0, jnp.where(rb == n_rb - 1, 2, 1))
    return pl.pallas_call(
        functools.partial(_na_attn_kernel, rows=rows),
        grid=(n_seq, NA_HEADS // hp, n_rb),
        in_specs=[pl.BlockSpec((qtok, LANES), lambda b, h, r: (qb0 + b * n_rb + r, h)),
                  pl.BlockSpec((seq_len, LANES), lambda b, h, r: (seq0 + b, h)),
                  pl.BlockSpec((seq_len, LANES), lambda b, h, r: (seq0 + b, h)),
                  pl.BlockSpec((None, past, LANES), lambda b, h, r: (b, 0, h)),
                  pl.BlockSpec((None, past, LANES), lambda b, h, r: (b, 0, h)),
                  pl.BlockSpec((hp, None, qtok, NA_KROWS * GRID_W), lambda b, h, r: (h, cls(r), 0, 0))],
        out_specs=pl.BlockSpec((qtok, LANES), lambda b, h, r: (b * n_rb + r, h)),
        out_shape=jax.ShapeDtypeStruct((n_seq * seq_len, NA_WIDTH), BF16),
        compiler_params=_cparams("parallel", "parallel", "arbitrary"),
        name="na_attn",
    )(q, k, v, k_ctx, v_ctx, bias_tab)


class HyPlan:
    def __init__(self, P):
        self.P, self.M = P, 2 * P
        self.K1 = P + 1
        self.K1P = -(-self.K1 // 8) * 8
        K1, K1P, M = self.K1, self.K1P, self.M
        ang = 2.0 * np.pi * ((np.arange(K1)[:, None] * np.arange(P)[None, :]) % M) / M
        ma = np.zeros((2 * K1P, P))
        ma[:K1] = np.cos(ang)
        ma[K1P:K1P + K1] = -np.sin(ang)
        self.ma = ma
        w = np.where((np.arange(K1) == 0) | (np.arange(K1) == P), 1.0, 2.0)[None, :]
        g = np.zeros((P, 2 * K1P))
        g[:, :K1] = w * np.cos(ang.T) / M
        g[:, K1P:K1P + K1] = -w * np.sin(ang.T) / M
        self.g = g
        sgn = np.zeros((K1P, LANES), np.float32)
        sgn[:K1] = np.where(np.arange(K1) % 2 == 0, 1.0, -1.0)[:, None]
        self.sgn = sgn


def _conv3_kernel(x_ref, w_ref, b_ref, o_ref):
    x = x_ref[...]
    L = x.shape[0]
    row = lax.broadcasted_iota(jnp.int32, x.shape, 0)
    xm = jnp.where(row == 0, 0.0, pltpu.roll(x, 1, axis=0))
    xp = jnp.where(row == L - 1, 0.0, pltpu.roll(x, L - 1, axis=0))
    o_ref[...] = b_ref[...] + w_ref[0:1, :] * xm + w_ref[1:2, :] * x + w_ref[2:3, :] * xp


def hy_conv3(proj3, seq0, n_seq, col0, conv_w, conv_b, width):
    _, L, _ = proj3.shape
    ct = min(width, max(256, HY_CONV_TILE_ELEMS // L))
    nj = width // ct
    return pl.pallas_call(
        _conv3_kernel,
        grid=(n_seq, 3, nj),
        in_specs=[pl.BlockSpec((None, L, ct), lambda b, s, j: (seq0 + b, 0, (col0 + s) * nj + j)),
                  pl.BlockSpec((HY_SHORT, ct), lambda b, s, j: (0, s * nj + j)),
                  pl.BlockSpec((1, ct), lambda b, s, j: (0, s * nj + j))],
        out_specs=pl.BlockSpec((None, None, L, ct), lambda b, s, j: (s, b, 0, j)),
        out_shape=jax.ShapeDtypeStruct((3, n_seq, L, width), F32),
        compiler_params=_cparams("parallel", "parallel", "parallel"),
        name="hy_conv3",
    )(proj3, conv_w, conv_b.reshape(1, -1))


def _hy_filter_kernel(z_ref, t_ref, w1_ref, b1_ref, w2_ref, b2_ref, fr_ref, w3f_ref, w3b_ref, dec_ref, o_ref, h_s):
    ct = w3f_ref.shape[1]

    @pl.when((pl.program_id(0) == 0) & (pl.program_id(1) == 0))
    def _():
        fr = fr_ref[...]
        w1 = _split_bf16(w1_ref[...])
        w2 = _split_bf16(w2_ref[...])
        for d in range(2):
            h = jnp.sin(fr * (_dot3_r(z_ref[d], *w1) + b1_ref[...]))
            h_s[d] = jnp.sin(fr * (_dot3_r(h, *w2) + b2_ref[...]))

    def branch(d, w3_ref):
        hv = _dot3_r(h_s[d], *_split_bf16(w3_ref[...]))
        t01 = jnp.concatenate([t_ref[d]] * (ct // LANES), axis=1)
        return hv * jnp.exp(-t01 * jnp.abs(dec_ref[d]))

    hf = branch(0, w3f_ref)
    hb = branch(1, w3b_ref)
    row = lax.broadcasted_iota(jnp.int32, hb.shape, 0)
    hb = jnp.where(row == 0, 0.0, hb)
    norm = jnp.sum(jnp.abs(hf), axis=0, keepdims=True) + jnp.sum(jnp.abs(hb), axis=0, keepdims=True)
    inv = 1.0 / norm
    o_ref[0] = hb * inv
    o_ref[1] = hf * inv


def hy_filters(L, w1, b1, w2, b2, w3, freq, decay, width):
    t = jnp.arange(L, dtype=F32)
    t01 = t / L
    bands = jnp.linspace(1e-4, HY_BANDS - 1, HY_BANDS, dtype=F32)
    ang = (2.0 * math.pi / L) * t[:, None] * bands[None, :]
    z = jnp.concatenate([t01[:, None], jnp.cos(ang), -jnp.sin(ang), jnp.zeros((L, HY_EMB_PAD - HY_EMB), F32)], axis=-1)
    rev = (L - jnp.arange(L)) % L
    z2 = jnp.stack([z, z[rev]], axis=0)
    tb = jnp.broadcast_to(t01[:, None], (L, LANES))
    t2 = jnp.stack([tb, tb[rev]], axis=0)
    w1p = jnp.concatenate([w1, jnp.zeros((HY_EMB_PAD - HY_EMB, HY_FFN), F32)], axis=0)
    ct = 256
    nj = width // ct
    full = lambda s: pl.BlockSpec(s, lambda o, j: (0,) * len(s))
    return pl.pallas_call(
        _hy_filter_kernel,
        grid=(HY_ORDER, nj),
        in_specs=[full((2, L, HY_EMB_PAD)), full((2, L, LANES)), full((HY_EMB_PAD, HY_FFN)), full((1, HY_FFN)),
                  full((HY_FFN, HY_FFN)), full((1, HY_FFN)), full((1, HY_FFN)),
                  pl.BlockSpec((HY_FFN, ct), lambda o, j: (0, (2 * o) * nj + j)),
                  pl.BlockSpec((HY_FFN, ct), lambda o, j: (0, (2 * o + 1) * nj + j)),
                  pl.BlockSpec((None, 2, 1, ct), lambda o, j: (o, 0, 0, j))],
        out_specs=pl.BlockSpec((None, 2, L, ct), lambda o, j: (o, 0, 0, j)),
        out_shape=jax.ShapeDtypeStruct((HY_ORDER, 2, L, width), F32),
        scratch_shapes=[pltpu.VMEM((2, L, HY_FFN), F32)],
        compiler_params=_cparams("arbitrary", "arbitrary"),
        name="hy_filter",
    )(z2, t2, w1p, b1.reshape(1, -1), w2, b2.reshape(1, -1),
      freq.reshape(1, -1), w3, w3, decay.reshape(HY_ORDER, 2, 1, width))


def _fwd_dft_kernel(x_ref, mhi_ref, mlo_ref, o_ref):
    o_ref[...] = _dot3(mhi_ref[...], mlo_ref[...], x_ref[...])


def hy_fwd_dft(xb, plan):
    NB, P, C = xb.shape
    mhi, mlo = _const_split(plan.ma)
    return pl.pallas_call(
        _fwd_dft_kernel,
        grid=(NB,),
        in_specs=[pl.BlockSpec((None, P, C), lambda b: (b, 0, 0)),
                  pl.BlockSpec(mhi.shape, lambda b: (0, 0)),
                  pl.BlockSpec(mlo.shape, lambda b: (0, 0))],
        out_specs=pl.BlockSpec((None, 2 * plan.K1P, C), lambda b: (b, 0, 0)),
        out_shape=jax.ShapeDtypeStruct((NB, 2 * plan.K1P, C), F32),
        compiler_params=_cparams("parallel"),
        name="hy_fwd_dft",
    )(xb, mhi, mlo)


def _row_chunk(k1p, ct):
    cap = max(8, HY_MUL_ACC_VREGS * 8 * LANES // ct)
    return max(r for r in range(8, cap + 1, 8) if k1p % r == 0)


def _block_mul_kernel(x_ref, s_ref, sgn_ref, y_ref, g_s, *, nb, k1p, ct):
    sgn = jnp.concatenate([sgn_ref[...]] * (ct // LANES), axis=1)
    sgn2 = jnp.concatenate([sgn, sgn], axis=0)
    for d in range(-(nb - 1), nb):
        g_s[d + nb - 1] = s_ref[d + nb] + sgn2 * s_ref[d + nb - 1]

    rc = _row_chunk(k1p, ct)

    def chunk(c, _):
        r0 = pl.multiple_of(c * rc, 8)
        for i in range(nb):
            def body(j, acc):
                yr, yi = acc
                xr = x_ref[j, pl.ds(r0, rc), :]
                xi = x_ref[j, pl.ds(k1p + r0, rc), :]
                gr = g_s[i - j + nb - 1, pl.ds(r0, rc), :]
                gi = g_s[i - j + nb - 1, pl.ds(k1p + r0, rc), :]
                return yr + (xr * gr - xi * gi), yi + (xr * gi + xi * gr)

            zero = jnp.zeros((rc, ct), F32)
            yr, yi = lax.fori_loop(0, nb, body, (zero, zero), unroll=True)
            y_ref[i, pl.ds(r0, rc), :] = yr
            y_ref[i, pl.ds(k1p + r0, rc), :] = yi
        return 0

    lax.fori_loop(0, k1p // rc, chunk, 0)


def hy_block_mul(x, s, plan, nb, order):
    NBt, k2, C = x.shape
    B = NBt // nb
    ct = min(C, max(LANES, HY_MUL_LANES // nb))
    nj = C // ct
    return pl.pallas_call(
        functools.partial(_block_mul_kernel, nb=nb, k1p=plan.K1P, ct=ct),
        grid=(nj, B),
        in_specs=[pl.BlockSpec((nb, k2, ct), lambda j, b: (b, 0, j)),
                  pl.BlockSpec((2 * nb, k2, ct), lambda j, b: (order, 0, j)),
                  pl.BlockSpec((plan.K1P, LANES), lambda j, b: (0, 0))],
        out_specs=pl.BlockSpec((nb, k2, ct), lambda j, b: (b, 0, j)),
        out_shape=jax.ShapeDtypeStruct((NBt, k2, C), F32),
        scratch_shapes=[pltpu.VMEM((2 * nb - 1, k2, ct), F32)],
        compiler_params=_cparams("parallel", "arbitrary"),
        name="hy_block_mul",
    )(x, s, jnp.asarray(plan.sgn))


def _inv_gate_kernel(y_ref, ghi_ref, glo_ref, z_ref, gate_ref, bias_ref, *rest, k1p, fused):
    if fused:
        s_ref, sgn_ref, o_ref, g_s = rest

        @pl.when(pl.program_id(0) == 0)
        def _():
            sgn = jnp.concatenate([sgn_ref[...]] * (s_ref.shape[2] // LANES), axis=1)
            g_s[...] = s_ref[1] + jnp.concatenate([sgn, sgn], axis=0) * s_ref[0]

        xr, xi = y_ref[:k1p, :], y_ref[k1p:, :]
        gr, gi = g_s[:k1p, :], g_s[k1p:, :]
        y = jnp.concatenate([xr * gr - xi * gi, xr * gi + xi * gr], axis=0)
    else:
        (o_ref,) = rest
        y = y_ref[...]
    conv = _dot3(ghi_ref[...], glo_ref[...], y)
    o_ref[...] = (gate_ref[...] * (conv + bias_ref[...] * z_ref[...])).astype(o_ref.dtype)


def hy_inv_gate(y, plan, zb, u, gate_idx, bias_n, out_dtype, s=None, order=0):
    NB, P, C = zb.shape
    k2 = 2 * plan.K1P
    ghi, glo = _const_split(plan.g)
    tok = pl.BlockSpec((None, P, C), lambda b: (b, 0, 0))
    fused = s is not None
    in_specs = [pl.BlockSpec((None, k2, C), lambda b: (b, 0, 0)),
                pl.BlockSpec(ghi.shape, lambda b: (0, 0)), pl.BlockSpec(glo.shape, lambda b: (0, 0)),
                tok, pl.BlockSpec((None, None, P, C), lambda b: (gate_idx, b, 0, 0)),
                pl.BlockSpec((1, C), lambda b: (0, 0))]
    args = [y, ghi, glo, zb, u, bias_n.reshape(1, C)]
    scratch = []
    if fused:
        in_specs += [pl.BlockSpec((2, k2, C), lambda b: (order, 0, 0)), pl.BlockSpec((plan.K1P, LANES), lambda b: (0, 0))]
        args += [s, jnp.asarray(plan.sgn)]
        scratch = [pltpu.VMEM((k2, C), F32)]
    return pl.pallas_call(
        functools.partial(_inv_gate_kernel, k1p=plan.K1P, fused=fused),
        grid=(NB,),
        in_specs=in_specs,
        out_specs=tok,
        out_shape=jax.ShapeDtypeStruct((NB, P, C), out_dtype),
        scratch_shapes=scratch,
        compiler_params=_cparams("arbitrary" if fused else "parallel"),
        name="hy_inv_gate",
    )(*args)


def hyena_mixer(proj3, seq0, n_seq, col0, conv_w, conv_b, w1, b1, w2, b2, w3, freq, decay, bias):
    _, L, _ = proj3.shape
    width, out_dtype = HY_WIDTH, BF16
    P = min(L, HY_BLOCK)
    nb = L // P
    plan = HyPlan(P)
    u = hy_conv3(proj3, seq0, n_seq, col0, conv_w, conv_b, width).reshape(3, n_seq * nb, P, width)
    klin = hy_filters(L, w1, b1, w2, b2, w3, freq, decay, width)
    s = hy_fwd_dft(klin.reshape(HY_ORDER * 2 * nb, P, width), plan)
    z = u[0]
    for n in range(HY_ORDER):
        x = hy_fwd_dft(z, plan)
        od = out_dtype if n == HY_ORDER - 1 else F32
        if nb == 1:
            z = hy_inv_gate(x, plan, z, u, 1 + n, bias[n], od, s=s, order=n)
        else:
            z = hy_inv_gate(hy_block_mul(x, s, plan, nb, n), plan, z, u, 1 + n, bias[n], od)
    return z.reshape(n_seq, L, width)


def _segments(B):
    assert B % CHAINS == 0 or CHAINS % B == 0
    return 1 if B % CHAINS == 0 else CHAINS // B


def kernel(x_prompt, x_sample, cache_na_k, cache_na_v, state_s5, state_rglru, c, c_ctx, norm_g, mod_w, mod_b, ffn_wg, ffn_wu, ffn_wd, even_w_in, even_w_out, s5_lam_re, s5_lam_im, s5_log_dt, s5_b_re, s5_b_im, s5_c_re, s5_c_im, s5_d, s5_glu_w, s5_glu_b, hy_conv_w, hy_conv_b, hy_w1, hy_b1, hy_w2, hy_b2, hy_w3, hy_freq, hy_decay, hy_bias, odd_w_in, odd_w_out, rg_conv_w, rg_conv_b, rg_wa, rg_ba, rg_wi, rg_bi, rg_lam, na_qn, na_kn, na_rpb):
    bp, lp, _ = x_prompt.shape
    bs, ls, _ = x_sample.shape
    n_p = bp * lp
    n_s = bs * ls
    n_tok = n_p + n_s
    assert n_p % ls == 0
    groups = ((0, n_p, bp, lp), (n_p, n_s, bs, ls))

    cond = jnp.concatenate([c_ctx[None], c, jnp.zeros((COND_ROWS - 1 - bs, D_MODEL), F32)], axis=0)
    m_all = adaln_all(cond, mod_w, mod_b)

    wg = ffn_wg.astype(BF16)
    wu = ffn_wu.astype(BF16)
    wd = ffn_wd.astype(BF16)
    w_in_e = even_w_in.astype(BF16)
    w_out_e = even_w_out.astype(BF16)
    w_in_o = odd_w_in.astype(BF16)
    w_out_o = odd_w_out.astype(BF16)

    x = jnp.concatenate([x_prompt.reshape(n_p, D_MODEL), x_sample.reshape(n_s, D_MODEL)], axis=0)
    s5_new, rg_new, k_new, v_new = [], [], [], []
    for li in range(DEPTH):
        x, h = ffn_block(x, m_all, norm_g, wg, wu, wd, li, 0, n_p, ls)
        if li % 2 == 0:
            ei = li // 2
            proj = in_proj(h, w_in_e, ei)
            prm = s5_params(s5_lam_re[ei], s5_lam_im[ei], s5_log_dt[ei], s5_b_re[ei], s5_b_im[ei],
                            s5_c_re[ei], s5_c_im[ei])
            ya, yb = [], []
            for gi, (t0, nt, B, L) in enumerate(groups):
                S = _segments(B)
                nc, Ls = B * S, L // S
                u3 = proj.reshape(n_tok // Ls, Ls, -1)
                if gi == 0:
                    y, hfin = s5_scan(u3, t0 // Ls, nc, *prm, jnp.zeros((2, nc, 2 * S5_STATE), F32))
                    fin = hfin.reshape(2, B, 2, S5_GROUPS, S5_P)
                    s5_new.append(jnp.transpose(fin, (1, 0, 3, 4, 2)))
                else:
                    h0 = jnp.transpose(state_s5[:, ei], (1, 0, 4, 2, 3)).reshape(2, B, 2 * S5_STATE)
                    y = s5_segmented(u3, t0 // Ls, prm, h0, B, S)
                ya.append(s5_glu(proj, t0, y.reshape(2, nt, S5_WIDTH), s5_d[ei].reshape(-1), s5_glu_w[ei], s5_glu_b[ei]))
                yb.append(hyena_mixer(proj.reshape(n_tok // L, L, -1), t0 // L, B, 1, hy_conv_w[ei], hy_conv_b[ei],
                                      hy_w1[ei], hy_b1[ei], hy_w2[ei], hy_b2[ei], hy_w3[ei], hy_freq[ei],
                                      hy_decay[ei], hy_bias[ei]).reshape(nt, HY_WIDTH))
            x = out_proj_residual(jnp.concatenate(ya, axis=0), jnp.concatenate(yb, axis=0), w_out_e, ei, x, m_all,
                                  li, n_p, ls)
        else:
            oi = li // 2
            proj = in_proj(h, w_in_o, oi)
            gprm = rglru_params(rg_wa[oi], rg_ba[oi], rg_wi[oi], rg_bi[oi])
            q, k, v, k_f32 = qkv_prep(proj, 2, na_qn[oi], na_kn[oi])
            yc, yd = [], []
            for gi, (t0, nt, B, L) in enumerate(groups):
                if gi == 0:
                    h0 = jnp.zeros((2, B, RG_WIDTH), F32)
                else:
                    h0 = jnp.transpose(state_rglru[:, oi], (1, 0, 2))
                y_c, hfin = rglru_mixer(proj, t0, rg_conv_w[oi], rg_conv_b[oi], gprm, rg_lam[oi], h0, B, L,
                                        _segments(B))
                yc.append(y_c)
                if gi == 0:
                    rg_new.append(jnp.transpose(hfin, (1, 0, 2)))
                    yd.append(ctx_attention(q, k, v, B, L))
                    k_new.append(k_f32[:nt].reshape(B, L, NA_HEADS, NA_HD))
                    v_new.append(proj[:nt, 4 * NA_WIDTH:].reshape(B, L, NA_HEADS, NA_HD))
                else:
                    past = cache_na_k.shape[2]
                    kc = cache_na_k[:, oi].reshape(B, past, NA_WIDTH).astype(BF16)
                    vc = cache_na_v[:, oi].reshape(B, past, NA_WIDTH).astype(BF16)
                    yd.append(na_attention(q, k, v, kc, vc, na_rpb[oi], t0, B, L))
            x = out_proj_residual(jnp.concatenate(yc, axis=0), jnp.concatenate(yd, axis=0), w_out_o, oi, x, m_all,
                                  li, n_p, ls)
        (x,) = ffn_block(x, m_all, norm_g, wg, wu, wd, li, 1, n_p, ls)

    y_prompt = x[:n_p].reshape(bp, lp, D_MODEL)
    y_sample = x[n_p:].reshape(bs, ls, D_MODEL)
    new_na_k = jnp.stack(k_new, axis=1)
    new_na_v = jnp.stack(v_new, axis=1)
    new_s5 = jnp.stack(s5_new, axis=1)
    new_rglru = jnp.stack(rg_new, axis=1)
    return (y_prompt, y_sample, new_na_k, new_na_v, new_s5, new_rglru)
```

```python
import functools
import math

import numpy as np
import jax
import jax.numpy as jnp
from jax import lax
from jax.experimental import pallas as pl
from jax.experimental.pallas import tpu as pltpu

F32 = jnp.float32
BF16 = jnp.bfloat16

D_MODEL = 2048
DEPTH = 4
N_MOD = 9
D_FF = 5632
EPS = 1e-6
MIX_HALF = D_MODEL // 2
GRID_W = 64
S5_WIDTH = MIX_HALF
S5_H = 16
S5_GROUPS = S5_WIDTH // S5_H
S5_P = 64
S5_STATE = S5_GROUPS * S5_P
S5_LAM_RE_MAX = -1e-4
HY_WIDTH = MIX_HALF
HY_ORDER = 2
HY_SHORT = 3
HY_BANDS = 16
HY_EMB = 1 + 2 * HY_BANDS
HY_EMB_PAD = 40
HY_FFN = 64
RG_WIDTH = MIX_HALF
RG_BLOCKS = 16
RG_BW = RG_WIDTH // RG_BLOCKS
RG_CONV = 4
RG_LEFT = RG_CONV // 2
RG_C = 8.0
NA_HD = 64
NA_HEADS = MIX_HALF // NA_HD
NA_WIDTH = NA_HEADS * NA_HD
NA_WIN_R = 8
NA_WIN_C = 16

VMEM_LIMIT_BYTES = 56 * 1024 * 1024
LANES = 128
CHAINS = 8
COND_ROWS = 8
TOKEN_TILE = 512
FF_TILE = 512
SCAN_T = 64
S5_COLS = 512
S5_BBLK = 256
S5_CBLK = 128
RG_GBLK = 256
HEAD_CHUNK = 256
NEG_BIG = -1e30
NA_QROWS = 4
NA_KROWS = NA_QROWS + NA_WIN_R
HY_BLOCK = 512
HY_CONV_TILE_ELEMS = 1024 * 1024
HY_MUL_LANES = 1024
HY_MUL_ACC_VREGS = 5


def _cparams(*sem):
    return pltpu.CompilerParams(dimension_semantics=sem, vmem_limit_bytes=VMEM_LIMIT_BYTES)


def _split_bf16(x):
    hi = x.astype(BF16)
    lo = (x - hi.astype(F32)).astype(BF16)
    return hi, lo


def _dot3(a_hi, a_lo, x):
    x_hi, x_lo = _split_bf16(x)
    return (jnp.dot(a_hi, x_hi, preferred_element_type=F32) + jnp.dot(a_lo, x_hi, preferred_element_type=F32)
            + jnp.dot(a_hi, x_lo, preferred_element_type=F32))


def _dot3_r(x, b_hi, b_lo):
    x_hi, x_lo = _split_bf16(x)
    return (jnp.dot(x_hi, b_hi, preferred_element_type=F32) + jnp.dot(x_hi, b_lo, preferred_element_type=F32)
            + jnp.dot(x_lo, b_hi, preferred_element_type=F32))


def _const_split(m):
    return _split_bf16(jnp.asarray(np.asarray(m, np.float32)))


def _gelu_tanh(x):
    return 0.5 * x * (1.0 + jnp.tanh(math.sqrt(2.0 / math.pi) * (x + 0.044715 * (x * x * x))))


def _adaln_kernel(cond_ref, w_ref, b_ref, o_ref):
    c = cond_ref[...]
    s = (c * jax.nn.sigmoid(c)).astype(BF16)
    o_ref[...] = jnp.dot(s, w_ref[...].astype(BF16), preferred_element_type=F32) + b_ref[...]


def adaln_all(cond, mod_w, mod_b):
    tn = 1024
    n_out = N_MOD * D_MODEL
    out = pl.pallas_call(
        _adaln_kernel,
        grid=(DEPTH, n_out // tn),
        in_specs=[
            pl.BlockSpec((COND_ROWS, D_MODEL), lambda l, j: (0, 0)),
            pl.BlockSpec((None, D_MODEL, tn), lambda l, j: (l, 0, j)),
            pl.BlockSpec((None, 1, tn), lambda l, j: (l, 0, j)),
        ],
        out_specs=pl.BlockSpec((None, COND_ROWS, tn), lambda l, j: (l, 0, j)),
        out_shape=jax.ShapeDtypeStruct((DEPTH, COND_ROWS, n_out), F32),
        compiler_params=_cparams("parallel", "parallel"),
        name="adaln",
    )(cond, mod_w, mod_b.reshape(DEPTH, 1, n_out))
    return out.reshape(DEPTH, COND_ROWS, N_MOD, D_MODEL)


def _rms_modulate(x, g, shift, scale):
    ms = jnp.mean(x * x, axis=-1, keepdims=True)
    return x * lax.rsqrt(ms + EPS) * g * (1.0 + scale) + shift


def _ffn_kernel(x_ref, m_ref, g_ref, wg_ref, wu_ref, wd_ref, o_ref, *rest,
                n_ff, in_rows, gate_row, g_row, next_rows, next_g_row):
    hn_ref = rest[0] if next_rows is not None else None
    h_s, acc_s = rest[-2:]
    k = pl.program_id(1)

    @pl.when(k == 0)
    def _():
        h = _rms_modulate(x_ref[...], g_ref[g_row:g_row + 1, :],
                          m_ref[in_rows[0]:in_rows[0] + 1, :], m_ref[in_rows[1]:in_rows[1] + 1, :])
        h_s[...] = h.astype(BF16)
        acc_s[...] = jnp.zeros_like(acc_s)

    h = h_s[...]
    gt = jnp.dot(h, wg_ref[...], preferred_element_type=F32)
    ut = jnp.dot(h, wu_ref[...], preferred_element_type=F32)
    a = (gt * jax.nn.sigmoid(gt) * ut).astype(BF16)
    acc_s[...] += jnp.dot(a, wd_ref[...], preferred_element_type=F32)

    @pl.when(k == n_ff - 1)
    def _():
        xn = x_ref[...] + 0.5 * m_ref[gate_row:gate_row + 1, :] * acc_s[...]
        o_ref[...] = xn
        if hn_ref is not None:
            hn = _rms_modulate(xn, g_ref[next_g_row:next_g_row + 1, :],
                               m_ref[next_rows[0]:next_rows[0] + 1, :], m_ref[next_rows[1]:next_rows[1] + 1, :])
            hn_ref[...] = hn.astype(BF16)


def _cond_row(i, tm, n_prompt_tok, sample_len):
    return jnp.where(i * tm < n_prompt_tok, 0, 1 + (i * tm - n_prompt_tok) // sample_len)


def ffn_block(x, m_all, norm_g, wg, wu, wd, li, half, n_prompt_tok, sample_len):
    n_tok = x.shape[0]
    tm, tf = TOKEN_TILE, FF_TILE
    n_ff = D_FF // tf
    if half == 0:
        cfg = dict(in_rows=(0, 1), gate_row=2, g_row=0, next_rows=(3, 4), next_g_row=1)
    else:
        cfg = dict(in_rows=(6, 7), gate_row=8, g_row=2, next_rows=None, next_g_row=None)
    row = functools.partial(_cond_row, tm=tm, n_prompt_tok=n_prompt_tok, sample_len=sample_len)
    tok = pl.BlockSpec((tm, D_MODEL), lambda i, k: (i, 0))
    out_specs = [tok]
    out_shape = [jax.ShapeDtypeStruct((n_tok, D_MODEL), F32)]
    if half == 0:
        out_specs.append(tok)
        out_shape.append(jax.ShapeDtypeStruct((n_tok, D_MODEL), BF16))
    return pl.pallas_call(
        functools.partial(_ffn_kernel, n_ff=n_ff, **cfg),
        grid=(n_tok // tm, n_ff),
        in_specs=[
            tok,
            pl.BlockSpec((None, None, N_MOD, D_MODEL), lambda i, k: (li, row(i), 0, 0)),
            pl.BlockSpec((None, 3, D_MODEL), lambda i, k: (li, 0, 0)),
            pl.BlockSpec((None, None, D_MODEL, tf), lambda i, k: (li, half, 0, k)),
            pl.BlockSpec((None, None, D_MODEL, tf), lambda i, k: (li, half, 0, k)),
            pl.BlockSpec((None, None, tf, D_MODEL), lambda i, k: (li, half, k, 0)),
        ],
        out_specs=out_specs,
        out_shape=out_shape,
        scratch_shapes=[pltpu.VMEM((tm, D_MODEL), BF16), pltpu.VMEM((tm, D_MODEL), F32)],
        compiler_params=_cparams("parallel", "arbitrary"),
        name=f"ffn_l{li}_h{half}",
    )(x, m_all, norm_g, wg, wu, wd)


def _proj_kernel(a_ref, w_ref, o_ref):
    o_ref[...] = jnp.dot(a_ref[...], w_ref[...], preferred_element_type=F32).astype(o_ref.dtype)


def in_proj(h, w, idx):
    n_tok = h.shape[0]
    n_out = w.shape[-1]
    tm, tn = TOKEN_TILE, n_out // 2
    return pl.pallas_call(
        _proj_kernel,
        grid=(n_out // tn, n_tok // tm),
        in_specs=[pl.BlockSpec((tm, D_MODEL), lambda j, i: (i, 0)),
                  pl.BlockSpec((None, D_MODEL, tn), lambda j, i: (idx, 0, j))],
        out_specs=pl.BlockSpec((tm, tn), lambda j, i: (i, j)),
        out_shape=jax.ShapeDtypeStruct((n_tok, n_out), F32),
        compiler_params=_cparams("parallel", "parallel"),
        name="in_proj",
    )(h, w)


def _out_proj_kernel(yap_ref, ybp_ref, yas_ref, ybs_ref, wa_ref, wb_ref, x_ref, m_ref, o_ref, *, n_p_tiles):
    def emit(ya_ref, yb_ref):
        y = (jnp.dot(ya_ref[...], wa_ref[...], preferred_element_type=F32)
             + jnp.dot(yb_ref[...], wb_ref[...], preferred_element_type=F32))
        o_ref[...] = x_ref[...] + m_ref[5:6, :] * y

    is_prompt = pl.program_id(0) < n_p_tiles
    pl.when(is_prompt)(lambda: emit(yap_ref, ybp_ref))
    pl.when(jnp.logical_not(is_prompt))(lambda: emit(yas_ref, ybs_ref))


def out_proj_residual(ya, yb, w, idx, x, m_all, li, n_prompt_tok, sample_len):
    n_tok = x.shape[0]
    tm = TOKEN_TILE
    n_p_tiles = n_prompt_tok // tm
    row = functools.partial(_cond_row, tm=tm, n_prompt_tok=n_prompt_tok, sample_len=sample_len)
    p_blk = pl.BlockSpec((tm, MIX_HALF), lambda i: (jnp.minimum(i, n_p_tiles - 1), 0))
    s_blk = pl.BlockSpec((tm, MIX_HALF), lambda i: (jnp.maximum(i - n_p_tiles, 0), 0))
    return pl.pallas_call(
        functools.partial(_out_proj_kernel, n_p_tiles=n_p_tiles),
        grid=(n_tok // tm,),
        in_specs=[p_blk, p_blk, s_blk, s_blk,
                  pl.BlockSpec((None, MIX_HALF, D_MODEL), lambda i: (idx, 0, 0)),
                  pl.BlockSpec((None, MIX_HALF, D_MODEL), lambda i: (idx, 1, 0)),
                  pl.BlockSpec((tm, D_MODEL), lambda i: (i, 0)),
                  pl.BlockSpec((None, None, N_MOD, D_MODEL), lambda i: (li, row(i), 0, 0))],
        out_specs=pl.BlockSpec((tm, D_MODEL), lambda i: (i, 0)),
        out_shape=jax.ShapeDtypeStruct((n_tok, D_MODEL), F32),
        compiler_params=_cparams("parallel"),
        name="out_proj",
    )(ya[0], yb[0], ya[1], yb[1], w, w, x, m_all)


def _s5_scan_kernel(u_ref, bw_ref, a_ref, cw_ref, h0_ref, *rest, n_t, t_tile, with_y):
    if with_y:
        y_ref, hfin_ref, br_s, bi_s, hr_s, hi_s = rest
    else:
        hfin_ref, br_s, bi_s, hr_s, hi_s = rest
    d = pl.program_id(0)
    jt = pl.program_id(2)
    rows = t_tile * CHAINS

    @pl.when(jt == 0)
    def _():
        hr_s[...] = h0_ref[:, :S5_STATE]
        hi_s[...] = h0_ref[:, S5_STATE:]

    u = jnp.swapaxes(u_ref[...], 0, 1).reshape(rows, S5_WIDTH).astype(BF16)
    n_bblk = S5_WIDTH // S5_BBLK
    ncol = S5_STATE // n_bblk
    for blk in range(n_bblk):
        bu = jnp.dot(u[:, blk * S5_BBLK:(blk + 1) * S5_BBLK], bw_ref[blk], preferred_element_type=F32)
        br_s[:, blk * ncol:(blk + 1) * ncol] = bu[:, :ncol]
        bi_s[:, blk * ncol:(blk + 1) * ncol] = bu[:, ncol:]

    for cc in range(S5_STATE // S5_COLS):
        cs = slice(cc * S5_COLS, (cc + 1) * S5_COLS)
        ar = a_ref[0, :, cs]
        ai = a_ref[1, :, cs]

        def step(i, carry):
            hr, hi = carry
            t = jnp.where(d == 0, i, t_tile - 1 - i)
            r0 = pl.multiple_of(t * CHAINS, CHAINS)
            nhr = ar * hr - ai * hi + br_s[pl.ds(r0, CHAINS), cs]
            nhi = ar * hi + ai * hr + bi_s[pl.ds(r0, CHAINS), cs]
            if with_y:
                br_s[pl.ds(r0, CHAINS), cs] = nhr
                bi_s[pl.ds(r0, CHAINS), cs] = nhi
            return nhr, nhi

        hr, hi = lax.fori_loop(0, t_tile, step, (hr_s[:, cs], hi_s[:, cs]), unroll=4)
        hr_s[:, cs] = hr
        hi_s[:, cs] = hi

    if with_y:
        n_cblk = S5_WIDTH // S5_CBLK
        kc = S5_STATE // n_cblk
        for i in range(n_cblk):
            yi = jnp.dot(br_s[:, i * kc:(i + 1) * kc].astype(BF16), cw_ref[0, i], preferred_element_type=F32)
            yi = yi + jnp.dot(bi_s[:, i * kc:(i + 1) * kc].astype(BF16), cw_ref[1, i], preferred_element_type=F32)
            y_ref[:, :, i * S5_CBLK:(i + 1) * S5_CBLK] = jnp.swapaxes(yi.reshape(t_tile, CHAINS, S5_CBLK), 0, 1)

    @pl.when(jt == n_t - 1)
    def _():
        hfin_ref[:, :S5_STATE] = hr_s[...]
        hfin_ref[:, S5_STATE:] = hi_s[...]


def s5_scan(u3, seq0, nc, bw, a_b, cw, h0, with_y=True):
    _, L, _ = u3.shape
    assert seq0 % CHAINS == 0 and nc % CHAINS == 0
    c0 = seq0 // CHAINS
    t_tile = min(SCAN_T, L)
    n_t = L // t_tile
    tix = lambda d, j: jnp.where(d == 0, j, n_t - 1 - j)
    fin_spec = pl.BlockSpec((None, CHAINS, 2 * S5_STATE), lambda d, c, j: (d, c, 0))
    fin_shape = jax.ShapeDtypeStruct((2, nc, 2 * S5_STATE), F32)
    out_specs, out_shape = [fin_spec], [fin_shape]
    if with_y:
        out_specs = [pl.BlockSpec((None, CHAINS, t_tile, S5_WIDTH), lambda d, c, j: (d, c, tix(d, j), 0))] + out_specs
        out_shape = [jax.ShapeDtypeStruct((2, nc, L, S5_WIDTH), F32)] + out_shape
    return pl.pallas_call(
        functools.partial(_s5_scan_kernel, n_t=n_t, t_tile=t_tile, with_y=with_y),
        grid=(2, nc // CHAINS, n_t),
        in_specs=[
            pl.BlockSpec((CHAINS, t_tile, S5_WIDTH), lambda d, c, j: (c0 + c, tix(d, j), 0)),
            pl.BlockSpec((None,) + bw.shape[1:], lambda d, c, j: (d, 0, 0, 0)),
            pl.BlockSpec((None,) + a_b.shape[1:], lambda d, c, j: (d, 0, 0, 0)),
            pl.BlockSpec((None,) + cw.shape[1:], lambda d, c, j: (d, 0, 0, 0, 0)),
            fin_spec,
        ],
        out_specs=out_specs,
        out_shape=out_shape,
        scratch_shapes=[pltpu.VMEM((t_tile * CHAINS, S5_STATE), F32), pltpu.VMEM((t_tile * CHAINS, S5_STATE), F32),
                        pltpu.VMEM((CHAINS, S5_STATE), F32), pltpu.VMEM((CHAINS, S5_STATE), F32)],
        compiler_params=_cparams("parallel", "parallel", "arbitrary"),
        name="s5_scan" if with_y else "s5_state",
    )(u3, bw, a_b, cw, h0)


def _seg_init_kernel(hl_ref, h0_ref, ar_ref, ai_ref, o_ref, *, n_seg, width, cplx):
    cr = h0_ref[:, :width]
    ci = h0_ref[:, width:] if cplx else None
    for k in range(n_seg):
        o_ref[k, :, :width] = cr
        ar = ar_ref[k]
        if cplx:
            o_ref[k, :, width:] = ci
            ai = ai_ref[k]
            cr, ci = (ar * cr - ai * ci + hl_ref[k, :, :width], ar * ci + ai * cr + hl_ref[k, :, width:])
        else:
            cr = ar * cr + hl_ref[k]


def seg_init(hl, h0, ar, ai=None):
    _, S, B, W2 = hl.shape
    cplx = ai is not None
    width = W2 // 2 if cplx else W2
    if ai is None:
        ai = ar
    aspec = pl.BlockSpec((None,) + ar.shape[1:], lambda d: (d, 0, 0, 0))
    return pl.pallas_call(
        functools.partial(_seg_init_kernel, n_seg=S, width=width, cplx=cplx),
        grid=(2,),
        in_specs=[pl.BlockSpec((None, S, B, W2), lambda d: (d, 0, 0, 0)),
                  pl.BlockSpec((None, B, W2), lambda d: (d, 0, 0)), aspec, aspec],
        out_specs=pl.BlockSpec((None, S, B, W2), lambda d: (d, 0, 0, 0)),
        out_shape=jax.ShapeDtypeStruct((2, S, B, W2), F32),
        compiler_params=_cparams("parallel"),
        name="seg_init",
    )(hl, h0, ar, ai)


def _to_proc_order(x, B, S):
    x = jnp.transpose(x.reshape(2, B, S, -1), (0, 2, 1, 3))
    return jnp.stack([x[0], x[1, ::-1]], axis=0)


def _from_proc_order(x, B, S):
    x = jnp.stack([x[0], x[1, ::-1]], axis=0)
    return jnp.transpose(x, (0, 2, 1, 3)).reshape(2, B * S, -1)


def _cpow(ar, ai, n):
    rr, ri = None, None
    br, bi = ar, ai
    while n:
        if n & 1:
            rr, ri = (br, bi) if rr is None else (rr * br - ri * bi, rr * bi + ri * br)
        n >>= 1
        if n:
            br, bi = br * br - bi * bi, 2.0 * br * bi
    return rr, ri


def s5_segmented(u3, seq0, prm, h0, B, S):
    bw, a_b, cw = prm
    Ls = u3.shape[1]
    zeros = jnp.zeros((2, B * S, 2 * S5_STATE), F32)
    (hloc,) = s5_scan(u3, seq0, B * S, bw, a_b, cw, zeros, with_y=False)
    pr, pi = _cpow(a_b[:, 0, 0], a_b[:, 1, 0], Ls)
    rep = lambda p: jnp.broadcast_to(p[:, None, None, :], (2, S, 1, S5_STATE))
    init = seg_init(_to_proc_order(hloc, B, S), h0, rep(pr), rep(pi))
    y, _ = s5_scan(u3, seq0, B * S, bw, a_b, cw, _from_proc_order(init, B, S), with_y=True)
    return y


def s5_params(lam_re, lam_im, log_dt, b_re, b_im, c_re, c_im):
    lr = jnp.minimum(lam_re, S5_LAM_RE_MAX)
    li = lam_im
    dt = jnp.exp(log_dt)[..., None]
    mag = jnp.exp(lr * dt)
    abr, abi = mag * jnp.cos(li * dt), mag * jnp.sin(li * dt)
    den = lr * lr + li * li
    cr = ((abr - 1.0) * lr + abi * li) / den
    ci = (abi * lr - (abr - 1.0) * li) / den
    bbr = cr[..., None] * b_re - ci[..., None] * b_im
    bbi = cr[..., None] * b_im + ci[..., None] * b_re
    n_bblk = S5_WIDTH // S5_BBLK
    gb = S5_GROUPS // n_bblk
    bb = jnp.stack([bbr, bbi], axis=1).reshape(2, 2, n_bblk, gb, S5_P, S5_H)
    bw = jnp.einsum('drbgph,gk->dbghrkp', bb, jnp.eye(gb, dtype=F32)).reshape(2, n_bblk, gb * S5_H, 2 * gb * S5_P)
    n_cblk = S5_WIDTH // S5_CBLK
    gc = S5_GROUPS // n_cblk
    cc = jnp.stack([c_re, -c_im], axis=1).reshape(2, 2, n_cblk, gc, S5_H, S5_P)
    cw = jnp.einsum('drbghp,gk->drbgpkh', cc, jnp.eye(gc, dtype=F32)).reshape(2, 2, n_cblk, gc * S5_P, gc * S5_H)
    a_b = jnp.stack([abr.reshape(2, S5_STATE), abi.reshape(2, S5_STATE)], axis=1)
    a_b = jnp.broadcast_to(a_b[:, :, None, :], (2, 2, CHAINS, S5_STATE))
    return bw.astype(BF16), a_b, cw.astype(BF16)


def _s5_glu_kernel(u_ref, y_ref, dsk_ref, w_ref, b_ref, o_ref):
    y = dsk_ref[...] * u_ref[...] + y_ref[0] + y_ref[1]
    z = _gelu_tanh(y)
    gate = jnp.dot(z.astype(BF16), w_ref[...], preferred_element_type=F32) + b_ref[...]
    o_ref[...] = (z * jax.nn.sigmoid(gate)).astype(o_ref.dtype)


def s5_glu(proj, tok0, y2d, d_skip, glu_w, glu_b):
    R = y2d.shape[1]
    tm = TOKEN_TILE
    r0 = tok0 // tm
    return pl.pallas_call(
        _s5_glu_kernel,
        grid=(R // tm,),
        in_specs=[pl.BlockSpec((tm, S5_WIDTH), lambda i: (r0 + i, 0)),
                  pl.BlockSpec((2, tm, S5_WIDTH), lambda i: (0, i, 0)),
                  pl.BlockSpec((1, S5_WIDTH), lambda i: (0, 0)),
                  pl.BlockSpec((S5_WIDTH, S5_WIDTH), lambda i: (0, 0)),
                  pl.BlockSpec((1, S5_WIDTH), lambda i: (0, 0))],
        out_specs=pl.BlockSpec((tm, S5_WIDTH), lambda i: (i, 0)),
        out_shape=jax.ShapeDtypeStruct((R, S5_WIDTH), BF16),
        compiler_params=_cparams("parallel"),
        name="s5_glu",
    )(proj, y2d, d_skip.reshape(1, S5_WIDTH), glu_w.astype(BF16), glu_b.reshape(1, S5_WIDTH))


def _rglru_kernel(xp_ref, xc_ref, xn_ref, cw_ref, cb_ref, wa_ref, ba_ref, wi_ref, bi_ref, lam_ref, h0_ref, *rest,
                  n_t, t_tile, seg, n_seg):
    if seg:
        h_ref, p_ref, hfin_ref, pfin_ref, a_s, b_s, hc_s, ht_s, pc_s, pt_s = rest
    else:
        h_ref, hfin_ref, a_s, b_s, hc_s, ht_s = rest
    d = pl.program_id(0)
    jt = pl.program_id(2)
    tj = jnp.where(d == 0, jt, n_t - 1 - jt)
    rows = t_tile * CHAINS

    @pl.when(jt == 0)
    def _():
        hc_s[...] = h0_ref[...]
        if seg:
            pc_s[...] = jnp.ones_like(pc_s)

    prev_in = jnp.swapaxes(xp_ref[...], 0, 1)[CHAINS - RG_LEFT:]
    next_in = jnp.swapaxes(xn_ref[...], 0, 1)[:RG_CONV - 1 - RG_LEFT]
    if seg:
        cidx = lax.broadcasted_iota(jnp.int32, (1, CHAINS, 1), 1) % n_seg
        first_t = tj == 0
        last_t = tj == n_t - 1
        prev_nb = pltpu.roll(prev_in, 1, axis=1)
        next_nb = pltpu.roll(next_in, CHAINS - 1, axis=1)
        prev = jnp.where(first_t, jnp.where(cidx == 0, 0.0, prev_nb), prev_in)
        nxt = jnp.where(last_t, jnp.where(cidx == n_seg - 1, 0.0, next_nb), next_in)
    else:
        prev = prev_in * (tj > 0).astype(F32)
        nxt = next_in * (tj < n_t - 1).astype(F32)
    xcat = jnp.concatenate([prev, jnp.swapaxes(xc_ref[...], 0, 1), nxt], axis=0)
    xc = cb_ref[...] + cw_ref[0:1, :] * xcat[0:t_tile]
    for j in range(1, RG_CONV):
        xc = xc + cw_ref[j:j + 1, :] * xcat[j:j + t_tile]
    x2 = xc.reshape(rows, RG_WIDTH)
    xb = x2.astype(BF16)

    ra, ia = [], []
    for j in range(RG_WIDTH // RG_GBLK):
        xs = xb[:, j * RG_GBLK:(j + 1) * RG_GBLK]
        ra.append(jnp.dot(xs, wa_ref[j], preferred_element_type=F32))
        ia.append(jnp.dot(xs, wi_ref[j], preferred_element_type=F32))
    r = jax.nn.sigmoid(jnp.concatenate(ra, axis=1) + ba_ref[...])
    ig = jax.nn.sigmoid(jnp.concatenate(ia, axis=1) + bi_ref[...])
    nl = -lam_ref[...]
    sp = jnp.maximum(nl, 0.0) + jnp.log1p(jnp.exp(-jnp.abs(nl)))
    log_a = (-RG_C) * r * sp
    a_s[...] = jnp.exp(log_a)
    th = jnp.tanh(log_a)
    one_minus_a2 = (-2.0) * th / (1.0 - th)
    b_s[...] = jnp.sqrt(one_minus_a2) * (ig * x2)

    def step(i, carry):
        t = jnp.where(d == 0, i, t_tile - 1 - i)
        r0 = pl.multiple_of(t * CHAINS, CHAINS)
        a = a_s[pl.ds(r0, CHAINS), :]
        if seg:
            h, p = carry
            h = a * h + b_s[pl.ds(r0, CHAINS), :]
            p = a * p
            ht_s[t] = h
            pt_s[t] = p
            return h, p
        h = a * carry + b_s[pl.ds(r0, CHAINS), :]
        ht_s[t] = h
        return h

    if seg:
        h, p = lax.fori_loop(0, t_tile, step, (hc_s[...], pc_s[...]), unroll=8)
        pc_s[...] = p
        p_ref[...] = jnp.swapaxes(pt_s[...], 0, 1)
    else:
        h = lax.fori_loop(0, t_tile, step, hc_s[...], unroll=8)
    hc_s[...] = h
    h_ref[...] = jnp.swapaxes(ht_s[...], 0, 1)

    @pl.when(jt == n_t - 1)
    def _():
        hfin_ref[...] = h
        if seg:
            pfin_ref[...] = p


def rglru_scan(x3, seq0, nc, conv_w, conv_b, wa, ba, wi, bi, lam, h0, n_seg=1):
    _, L, _ = x3.shape
    seg = n_seg > 1
    assert seq0 % CHAINS == 0 and nc % CHAINS == 0 and (not seg or nc == CHAINS)
    c0 = seq0 // CHAINS
    t_tile = min(SCAN_T, L)
    n_t = L // t_tile
    tix = lambda d, j: jnp.where(d == 0, j, n_t - 1 - j)
    per = t_tile // CHAINS
    n_hb = L // CHAINS
    if seg:
        prev_ix = lambda d, c, j: (c0 + c, (tix(d, j) * per - 1) % n_hb, 0)
        next_ix = lambda d, c, j: (c0 + c, ((tix(d, j) + 1) * per) % n_hb, 0)
    else:
        prev_ix = lambda d, c, j: (c0 + c, jnp.maximum(tix(d, j) * per - 1, 0), 0)
        next_ix = lambda d, c, j: (c0 + c, jnp.minimum((tix(d, j) + 1) * per, n_hb - 1), 0)
    hspec = pl.BlockSpec((None, CHAINS, t_tile, RG_WIDTH), lambda d, c, j: (d, c, tix(d, j), 0))
    fspec = pl.BlockSpec((None, CHAINS, RG_WIDTH), lambda d, c, j: (d, c, 0))
    hshape = jax.ShapeDtypeStruct((2, nc, L, RG_WIDTH), F32)
    fshape = jax.ShapeDtypeStruct((2, nc, RG_WIDTH), F32)
    tile = pltpu.VMEM((t_tile, CHAINS, RG_WIDTH), F32)
    scratch = [pltpu.VMEM((t_tile * CHAINS, RG_WIDTH), F32), pltpu.VMEM((t_tile * CHAINS, RG_WIDTH), F32),
               pltpu.VMEM((CHAINS, RG_WIDTH), F32), tile]
    if seg:
        out_specs, out_shape = [hspec, hspec, fspec, fspec], [hshape, hshape, fshape, fshape]
        scratch += [pltpu.VMEM((CHAINS, RG_WIDTH), F32), tile]
    else:
        out_specs, out_shape = [hspec, fspec], [hshape, fshape]
    halo = lambda ix: pl.BlockSpec((CHAINS, CHAINS, RG_WIDTH), ix)
    return pl.pallas_call(
        functools.partial(_rglru_kernel, n_t=n_t, t_tile=t_tile, seg=seg, n_seg=n_seg),
        grid=(2, nc // CHAINS, n_t),
        in_specs=[
            halo(prev_ix),
            pl.BlockSpec((CHAINS, t_tile, RG_WIDTH), lambda d, c, j: (c0 + c, tix(d, j), 0)),
            halo(next_ix),
            pl.BlockSpec((RG_CONV, RG_WIDTH), lambda d, c, j: (0, 0)),
            pl.BlockSpec((1, RG_WIDTH), lambda d, c, j: (0, 0)),
            pl.BlockSpec((None,) + wa.shape[1:], lambda d, c, j: (d, 0, 0, 0)),
            pl.BlockSpec((None, 1, RG_WIDTH), lambda d, c, j: (d, 0, 0)),
            pl.BlockSpec((None,) + wi.shape[1:], lambda d, c, j: (d, 0, 0, 0)),
            pl.BlockSpec((None, 1, RG_WIDTH), lambda d, c, j: (d, 0, 0)),
            pl.BlockSpec((None, 1, RG_WIDTH), lambda d, c, j: (d, 0, 0)),
            fspec,
        ],
        out_specs=out_specs,
        out_shape=out_shape,
        scratch_shapes=scratch,
        compiler_params=_cparams("parallel", "parallel", "arbitrary"),
        name="rglru_seg" if seg else "rglru_scan",
    )(x3, x3, x3, conv_w, conv_b.reshape(1, RG_WIDTH), wa, ba, wi, bi, lam.reshape(2, 1, RG_WIDTH), h0)


def rglru_params(w_a, b_a, w_i, b_i):
    nb = RG_WIDTH // RG_GBLK
    per = RG_GBLK // RG_BW
    eye = jnp.eye(per, dtype=F32)

    def bd(w):
        w = w.reshape(2, nb, per, RG_BW, RG_BW)
        return jnp.einsum('dbnce,nm->dbncme', w, eye).reshape(2, nb, RG_GBLK, RG_GBLK).astype(BF16)

    return bd(w_a), b_a.reshape(2, 1, RG_WIDTH), bd(w_i), b_i.reshape(2, 1, RG_WIDTH)


def _rg_combine_kernel(h_ref, g_ref, *rest, seg):
    if seg:
        p_ref, i_ref, o_ref = rest
        hsum = h_ref[0] + p_ref[0] * i_ref[0] + h_ref[1] + p_ref[1] * i_ref[1]
    else:
        (o_ref,) = rest
        hsum = h_ref[0] + h_ref[1]
    o_ref[...] = (hsum * _gelu_tanh(g_ref[...])).astype(o_ref.dtype)


def rg_combine(h2, proj, tok0, col_blk, p2=None, init=None, seg_len=None):
    R = h2.shape[1]
    seg = p2 is not None
    tm = math.gcd(TOKEN_TILE, seg_len) if seg else TOKEN_TILE
    r0 = tok0 // tm
    hspec = pl.BlockSpec((2, tm, RG_WIDTH), lambda i: (0, i, 0))
    in_specs = [hspec, pl.BlockSpec((tm, RG_WIDTH), lambda i: (r0 + i, col_blk))]
    args = [h2, proj]
    if seg:
        in_specs += [hspec, pl.BlockSpec((2, None, 1, RG_WIDTH), lambda i: (0, (i * tm) // seg_len, 0, 0))]
        args += [p2, init.reshape(2, -1, 1, RG_WIDTH)]
    return pl.pallas_call(
        functools.partial(_rg_combine_kernel, seg=seg),
        grid=(R // tm,),
        in_specs=in_specs,
        out_specs=pl.BlockSpec((tm, RG_WIDTH), lambda i: (i, 0)),
        out_shape=jax.ShapeDtypeStruct((R, RG_WIDTH), BF16),
        compiler_params=_cparams("parallel"),
        name="rg_combine",
    )(*args)


def rglru_mixer(proj, tok0, conv_w, conv_b, gprm, lam, h0, B, L, S):
    nc, Ls = B * S, L // S
    x3 = proj.reshape(proj.shape[0] // Ls, Ls, -1)
    seq0 = tok0 // Ls
    if S == 1:
        h, hfin = rglru_scan(x3, seq0, nc, conv_w, conv_b, *gprm, lam, h0)
        return rg_combine(h.reshape(2, nc * Ls, RG_WIDTH), proj, tok0, 1), hfin
    zeros = jnp.zeros((2, nc, RG_WIDTH), F32)
    h, p, hfin, pfin = rglru_scan(x3, seq0, nc, conv_w, conv_b, *gprm, lam, zeros, n_seg=S)
    init = seg_init(_to_proc_order(hfin, B, S), h0, _to_proc_order(pfin, B, S))
    init = _from_proc_order(init, B, S)
    y = rg_combine(h.reshape(2, nc * Ls, RG_WIDTH), proj, tok0, 1, p2=p.reshape(2, nc * Ls, RG_WIDTH), init=init,
                   seg_len=Ls)
    return y, hfin


def _head_rms(x, ones_bd, gain):
    sq = x * x
    hi, lo = _split_bf16(sq)
    parts = []
    for j in range(NA_WIDTH // HEAD_CHUNK):
        cs = slice(j * HEAD_CHUNK, (j + 1) * HEAD_CHUNK)
        parts.append(jnp.dot(hi[:, cs], ones_bd, preferred_element_type=F32)
                     + jnp.dot(lo[:, cs], ones_bd, preferred_element_type=F32))
    ms = jnp.concatenate(parts, axis=1) * (1.0 / NA_HD)
    return x * lax.rsqrt(ms + EPS) * gain


def _qkv_prep_kernel(q_ref, k_ref, v_ref, ones_ref, gq_ref, gk_ref, qo_ref, ko_ref, vo_ref, kf_ref):
    ones_bd = ones_ref[...]
    qo_ref[...] = _head_rms(q_ref[...], ones_bd, gq_ref[...]).astype(BF16)
    kn = _head_rms(k_ref[...], ones_bd, gk_ref[...])
    ko_ref[...] = kn.astype(BF16)
    kf_ref[...] = kn
    vo_ref[...] = v_ref[...].astype(BF16)


def qkv_prep(proj, col0, qn_g, kn_g):
    n_tok = proj.shape[0]
    tm = TOKEN_TILE
    ones_bd = jnp.kron(jnp.eye(HEAD_CHUNK // NA_HD, dtype=F32), jnp.ones((NA_HD, NA_HD), F32)).astype(BF16)
    gq = jnp.tile(qn_g, NA_HEADS).reshape(1, NA_WIDTH)
    gk = jnp.tile(kn_g, NA_HEADS).reshape(1, NA_WIDTH)
    blk = lambda c: pl.BlockSpec((tm, NA_WIDTH), lambda i: (i, c))
    full = lambda s: pl.BlockSpec(s, lambda i: (0, 0))
    return pl.pallas_call(
        _qkv_prep_kernel,
        grid=(n_tok // tm,),
        in_specs=[blk(col0), blk(col0 + 1), blk(col0 + 2), full((HEAD_CHUNK, HEAD_CHUNK)),
                  full((1, NA_WIDTH)), full((1, NA_WIDTH))],
        out_specs=[blk(0), blk(0), blk(0), blk(0)],
        out_shape=[jax.ShapeDtypeStruct((n_tok, NA_WIDTH), BF16)] * 3 + [jax.ShapeDtypeStruct((n_tok, NA_WIDTH), F32)],
        compiler_params=_cparams("parallel"),
        name="qkv_prep",
    )(proj, proj, proj, ones_bd, gq, gk)


_NT = (((1,), (1,)), ((), ()))


def _ctx_attn_kernel(q_ref, k_ref, v_ref, o_ref):
    scale = NA_HD ** -0.5
    outs = []
    for h in range(NA_HEADS):
        hs = slice(h * NA_HD, (h + 1) * NA_HD)
        s = lax.dot_general(q_ref[:, hs], k_ref[:, hs], _NT, preferred_element_type=F32) * scale
        m = jnp.max(s, axis=-1, keepdims=True)
        e = jnp.exp(s - m)
        l = jnp.sum(e, axis=-1, keepdims=True)
        o = jnp.dot(e.astype(BF16), v_ref[:, hs], preferred_element_type=F32)
        outs.append(o / l)
        if h % 2 == 1:
            o_ref[:, (h - 1) * NA_HD:(h + 1) * NA_HD] = jnp.concatenate(outs, axis=1).astype(o_ref.dtype)
            outs = []


def ctx_attention(q, k, v, n_seq, seq_len):
    blk = pl.BlockSpec((seq_len, NA_WIDTH), lambda b: (b, 0))
    return pl.pallas_call(
        _ctx_attn_kernel,
        grid=(n_seq,),
        in_specs=[blk, blk, blk],
        out_specs=blk,
        out_shape=jax.ShapeDtypeStruct((n_seq * seq_len, NA_WIDTH), BF16),
        compiler_params=_cparams("parallel"),
        name="ctx_attn",
    )(q, k, v)


def _na_attn_kernel(q_ref, k_ref, v_ref, kc_ref, vc_ref, bias_ref, o_ref, *, rows):
    rb = pl.program_id(2)
    ws = jnp.clip(rb * NA_QROWS - NA_WIN_R // 2, 0, rows - NA_KROWS)
    k0 = pl.multiple_of(ws * GRID_W, GRID_W)
    nloc = NA_KROWS * GRID_W
    scale = NA_HD ** -0.5
    outs = []
    for h in range(LANES // NA_HD):
        hs = slice(h * NA_HD, (h + 1) * NA_HD)
        qh = q_ref[:, hs]
        s_loc = lax.dot_general(qh, k_ref[pl.ds(k0, nloc), hs], _NT, preferred_element_type=F32) * scale + bias_ref[h]
        s_ctx = lax.dot_general(qh, kc_ref[:, hs], _NT, preferred_element_type=F32) * scale
        m = jnp.maximum(jnp.max(s_loc, axis=-1, keepdims=True), jnp.max(s_ctx, axis=-1, keepdims=True))
        e_loc = jnp.exp(s_loc - m)
        e_ctx = jnp.exp(s_ctx - m)
        l = jnp.sum(e_loc, axis=-1, keepdims=True) + jnp.sum(e_ctx, axis=-1, keepdims=True)
        o = (jnp.dot(e_loc.astype(BF16), v_ref[pl.ds(k0, nloc), hs], preferred_element_type=F32)
             + jnp.dot(e_ctx.astype(BF16), vc_ref[:, hs], preferred_element_type=F32))
        outs.append(o / l)
    o_ref[...] = jnp.concatenate(outs, axis=1).astype(o_ref.dtype)


def na_bias_table(rpb, rows):
    qc = np.arange(GRID_W)
    kc = np.arange(GRID_W)
    cs = np.clip(qc - NA_WIN_C // 2, 0, GRID_W - NA_WIN_C)
    cvalid = (kc[None, :] >= cs[:, None]) & (kc[None, :] < cs[:, None] + NA_WIN_C)
    colrel = np.clip(kc[None, :] - qc[:, None] + NA_WIN_C - 1, 0, 2 * NA_WIN_C - 2)
    toep = rpb[:, :, colrel]
    rowrel = np.zeros((3, NA_QROWS, NA_KROWS), np.int32)
    rvalid = np.zeros((3, NA_QROWS, NA_KROWS), bool)
    for cls, r0 in enumerate((0, NA_WIN_R // 2, rows - NA_QROWS)):
        ws = min(max(r0 - NA_WIN_R // 2, 0), rows - NA_KROWS)
        for qr in range(NA_QROWS):
            r = r0 + qr
            rs = min(max(r - NA_WIN_R // 2, 0), rows - NA_WIN_R)
            for j in range(NA_KROWS):
                kr = ws + j
                rvalid[cls, qr, j] = rs <= kr < rs + NA_WIN_R
                rowrel[cls, qr, j] = min(max(kr - r + NA_WIN_R - 1, 0), 2 * NA_WIN_R - 2)
    tab = toep[:, rowrel]
    valid = rvalid[:, :, :, None, None] & cvalid[None, None, None]
    tab = jnp.where(valid[None], tab, NEG_BIG)
    tab = jnp.transpose(tab, (0, 1, 2, 4, 3, 5))
    return tab.reshape(rpb.shape[0], 3, NA_QROWS * GRID_W, NA_KROWS * GRID_W)


def na_attention(q, k, v, k_ctx, v_ctx, rpb, tok0, n_seq, seq_len):
    rows = seq_len // GRID_W
    past = k_ctx.shape[1]
    qtok = NA_QROWS * GRID_W
    n_rb = rows // NA_QROWS
    assert tok0 % seq_len == 0 and rows % NA_QROWS == 0 and rows >= NA_KROWS
    seq0 = tok0 // seq_len
    qb0 = tok0 // qtok
    hp = LANES // NA_HD
    bias_tab = na_bias_table(rpb, rows)
    cls = lambda rb: jnp.where(rb == 0, 0, jnp.where(rb == n_rb - 1, 2, 1))
    return pl.pallas_call(
        functools.partial(_na_attn_kernel, rows=rows),
        grid=(n_seq, NA_HEADS // hp, n_rb),
        in_specs=[pl.BlockSpec((qtok, LANES), lambda b, h, r: (qb0 + b * n_rb + r, h)),
                  pl.BlockSpec((seq_len, LANES), lambda b, h, r: (seq0 + b, h)),
                  pl.BlockSpec((seq_len, LANES), lambda b, h, r: (seq0 + b, h)),
                  pl.BlockSpec((None, past, LANES), lambda b, h, r: (b, 0, h)),
                  pl.BlockSpec((None, past, LANES), lambda b, h, r: (b, 0, h)),
                  pl.BlockSpec((hp, None, qtok, NA_KROWS * GRID_W), lambda b, h, r: (h, cls(r), 0, 0))],
        out_specs=pl.BlockSpec((qtok, LANES), lambda b, h, r: (b * n_rb + r, h)),
        out_shape=jax.ShapeDtypeStruct((n_seq * seq_len, NA_WIDTH), BF16),
        compiler_params=_cparams("parallel", "parallel", "arbitrary"),
        name="na_attn",
    )(q, k, v, k_ctx, v_ctx, bias_tab)


class HyPlan:
    def __init__(self, P):
        self.P, self.M = P, 2 * P
        self.K1 = P + 1
        self.K1P = -(-self.K1 // 8) * 8
        K1, K1P, M = self.K1, self.K1P, self.M
        ang = 2.0 * np.pi * ((np.arange(K1)[:, None] * np.arange(P)[None, :]) % M) / M
        ma = np.zeros((2 * K1P, P))
        ma[:K1] = np.cos(ang)
        ma[K1P:K1P + K1] = -np.sin(ang)
        self.ma = ma
        w = np.where((np.arange(K1) == 0) | (np.arange(K1) == P), 1.0, 2.0)[None, :]
        g = np.zeros((P, 2 * K1P))
        g[:, :K1] = w * np.cos(ang.T) / M
        g[:, K1P:K1P + K1] = -w * np.sin(ang.T) / M
        self.g = g
        sgn = np.zeros((K1P, LANES), np.float32)
        sgn[:K1] = np.where(np.arange(K1) % 2 == 0, 1.0, -1.0)[:, None]
        self.sgn = sgn


def _conv3_kernel(x_ref, w_ref, b_ref, o_ref):
    x = x_ref[...]
    L = x.shape[0]
    row = lax.broadcasted_iota(jnp.int32, x.shape, 0)
    xm = jnp.where(row == 0, 0.0, pltpu.roll(x, 1, axis=0))
    xp = jnp.where(row == L - 1, 0.0, pltpu.roll(x, L - 1, axis=0))
    o_ref[...] = b_ref[...] + w_ref[0:1, :] * xm + w_ref[1:2, :] * x + w_ref[2:3, :] * xp


def hy_conv3(proj3, seq0, n_seq, col0, conv_w, conv_b, width):
    _, L, _ = proj3.shape
    ct = min(width, max(256, HY_CONV_TILE_ELEMS // L))
    nj = width // ct
    return pl.pallas_call(
        _conv3_kernel,
        grid=(n_seq, 3, nj),
        in_specs=[pl.BlockSpec((None, L, ct), lambda b, s, j: (seq0 + b, 0, (col0 + s) * nj + j)),
                  pl.BlockSpec((HY_SHORT, ct), lambda b, s, j: (0, s * nj + j)),
                  pl.BlockSpec((1, ct), lambda b, s, j: (0, s * nj + j))],
        out_specs=pl.BlockSpec((None, None, L, ct), lambda b, s, j: (s, b, 0, j)),
        out_shape=jax.ShapeDtypeStruct((3, n_seq, L, width), F32),
        compiler_params=_cparams("parallel", "parallel", "parallel"),
        name="hy_conv3",
    )(proj3, conv_w, conv_b.reshape(1, -1))


def _hy_filter_kernel(z_ref, t_ref, w1_ref, b1_ref, w2_ref, b2_ref, fr_ref, w3f_ref, w3b_ref, dec_ref, o_ref, h_s):
    ct = w3f_ref.shape[1]

    @pl.when((pl.program_id(0) == 0) & (pl.program_id(1) == 0))
    def _():
        fr = fr_ref[...]
        w1 = _split_bf16(w1_ref[...])
        w2 = _split_bf16(w2_ref[...])
        for d in range(2):
            h = jnp.sin(fr * (_dot3_r(z_ref[d], *w1) + b1_ref[...]))
            h_s[d] = jnp.sin(fr * (_dot3_r(h, *w2) + b2_ref[...]))

    def branch(d, w3_ref):
        hv = _dot3_r(h_s[d], *_split_bf16(w3_ref[...]))
        t01 = jnp.concatenate([t_ref[d]] * (ct // LANES), axis=1)
        return hv * jnp.exp(-t01 * jnp.abs(dec_ref[d]))

    hf = branch(0, w3f_ref)
    hb = branch(1, w3b_ref)
    row = lax.broadcasted_iota(jnp.int32, hb.shape, 0)
    hb = jnp.where(row == 0, 0.0, hb)
    norm = jnp.sum(jnp.abs(hf), axis=0, keepdims=True) + jnp.sum(jnp.abs(hb), axis=0, keepdims=True)
    inv = 1.0 / norm
    o_ref[0] = hb * inv
    o_ref[1] = hf * inv


def hy_filters(L, w1, b1, w2, b2, w3, freq, decay, width):
    t = jnp.arange(L, dtype=F32)
    t01 = t / L
    bands = jnp.linspace(1e-4, HY_BANDS - 1, HY_BANDS, dtype=F32)
    ang = (2.0 * math.pi / L) * t[:, None] * bands[None, :]
    z = jnp.concatenate([t01[:, None], jnp.cos(ang), -jnp.sin(ang), jnp.zeros((L, HY_EMB_PAD - HY_EMB), F32)], axis=-1)
    rev = (L - jnp.arange(L)) % L
    z2 = jnp.stack([z, z[rev]], axis=0)
    tb = jnp.broadcast_to(t01[:, None], (L, LANES))
    t2 = jnp.stack([tb, tb[rev]], axis=0)
    w1p = jnp.concatenate([w1, jnp.zeros((HY_EMB_PAD - HY_EMB, HY_FFN), F32)], axis=0)
    ct = 256
    nj = width // ct
    full = lambda s: pl.BlockSpec(s, lambda o, j: (0,) * len(s))
    return pl.pallas_call(
        _hy_filter_kernel,
        grid=(HY_ORDER, nj),
        in_specs=[full((2, L, HY_EMB_PAD)), full((2, L, LANES)), full((HY_EMB_PAD, HY_FFN)), full((1, HY_FFN)),
                  full((HY_FFN, HY_FFN)), full((1, HY_FFN)), full((1, HY_FFN)),
                  pl.BlockSpec((HY_FFN, ct), lambda o, j: (0, (2 * o) * nj + j)),
                  pl.BlockSpec((HY_FFN, ct), lambda o, j: (0, (2 * o + 1) * nj + j)),
                  pl.BlockSpec((None, 2, 1, ct), lambda o, j: (o, 0, 0, j))],
        out_specs=pl.BlockSpec((None, 2, L, ct), lambda o, j: (o, 0, 0, j)),
        out_shape=jax.ShapeDtypeStruct((HY_ORDER, 2, L, width), F32),
        scratch_shapes=[pltpu.VMEM((2, L, HY_FFN), F32)],
        compiler_params=_cparams("arbitrary", "arbitrary"),
        name="hy_filter",
    )(z2, t2, w1p, b1.reshape(1, -1), w2, b2.reshape(1, -1),
      freq.reshape(1, -1), w3, w3, decay.reshape(HY_ORDER, 2, 1, width))


def _fwd_dft_kernel(x_ref, mhi_ref, mlo_ref, o_ref):
    o_ref[...] = _dot3(mhi_ref[...], mlo_ref[...], x_ref[...])


def hy_fwd_dft(xb, plan):
    NB, P, C = xb.shape
    mhi, mlo = _const_split(plan.ma)
    return pl.pallas_call(
        _fwd_dft_kernel,
        grid=(NB,),
        in_specs=[pl.BlockSpec((None, P, C), lambda b: (b, 0, 0)),
                  pl.BlockSpec(mhi.shape, lambda b: (0, 0)),
                  pl.BlockSpec(mlo.shape, lambda b: (0, 0))],
        out_specs=pl.BlockSpec((None, 2 * plan.K1P, C), lambda b: (b, 0, 0)),
        out_shape=jax.ShapeDtypeStruct((NB, 2 * plan.K1P, C), F32),
        compiler_params=_cparams("parallel"),
        name="hy_fwd_dft",
    )(xb, mhi, mlo)


def _row_chunk(k1p, ct):
    cap = max(8, HY_MUL_ACC_VREGS * 8 * LANES // ct)
    return max(r for r in range(8, cap + 1, 8) if k1p % r == 0)


def _block_mul_kernel(x_ref, s_ref, sgn_ref, y_ref, g_s, *, nb, k1p, ct):
    sgn = jnp.concatenate([sgn_ref[...]] * (ct // LANES), axis=1)
    sgn2 = jnp.concatenate([sgn, sgn], axis=0)
    for d in range(-(nb - 1), nb):
        g_s[d + nb - 1] = s_ref[d + nb] + sgn2 * s_ref[d + nb - 1]

    rc = _row_chunk(k1p, ct)

    def chunk(c, _):
        r0 = pl.multiple_of(c * rc, 8)
        for i in range(nb):
            def body(j, acc):
                yr, yi = acc
                xr = x_ref[j, pl.ds(r0, rc), :]
                xi = x_ref[j, pl.ds(k1p + r0, rc), :]
                gr = g_s[i - j + nb - 1, pl.ds(r0, rc), :]
                gi = g_s[i - j + nb - 1, pl.ds(k1p + r0, rc), :]
                return yr + (xr * gr - xi * gi), yi + (xr * gi + xi * gr)

            zero = jnp.zeros((rc, ct), F32)
            yr, yi = lax.fori_loop(0, nb, body, (zero, zero), unroll=True)
            y_ref[i, pl.ds(r0, rc), :] = yr
            y_ref[i, pl.ds(k1p + r0, rc), :] = yi
        return 0

    lax.fori_loop(0, k1p // rc, chunk, 0)


def hy_block_mul(x, s, plan, nb, order):
    NBt, k2, C = x.shape
    B = NBt // nb
    ct = min(C, max(LANES, HY_MUL_LANES // nb))
    nj = C // ct
    return pl.pallas_call(
        functools.partial(_block_mul_kernel, nb=nb, k1p=plan.K1P, ct=ct),
        grid=(nj, B),
        in_specs=[pl.BlockSpec((nb, k2, ct), lambda j, b: (b, 0, j)),
                  pl.BlockSpec((2 * nb, k2, ct), lambda j, b: (order, 0, j)),
                  pl.BlockSpec((plan.K1P, LANES), lambda j, b: (0, 0))],
        out_specs=pl.BlockSpec((nb, k2, ct), lambda j, b: (b, 0, j)),
        out_shape=jax.ShapeDtypeStruct((NBt, k2, C), F32),
        scratch_shapes=[pltpu.VMEM((2 * nb - 1, k2, ct), F32)],
        compiler_params=_cparams("parallel", "arbitrary"),
        name="hy_block_mul",
    )(x, s, jnp.asarray(plan.sgn))


def _inv_gate_kernel(y_ref, ghi_ref, glo_ref, z_ref, gate_ref, bias_ref, *rest, k1p, fused):
    if fused:
        s_ref, sgn_ref, o_ref, g_s = rest

        @pl.when(pl.program_id(0) == 0)
        def _():
            sgn = jnp.concatenate([sgn_ref[...]] * (s_ref.shape[2] // LANES), axis=1)
            g_s[...] = s_ref[1] + jnp.concatenate([sgn, sgn], axis=0) * s_ref[0]

        xr, xi = y_ref[:k1p, :], y_ref[k1p:, :]
        gr, gi = g_s[:k1p, :], g_s[k1p:, :]
        y = jnp.concatenate([xr * gr - xi * gi, xr * gi + xi * gr], axis=0)
    else:
        (o_ref,) = rest
        y = y_ref[...]
    conv = _dot3(ghi_ref[...], glo_ref[...], y)
    o_ref[...] = (gate_ref[...] * (conv + bias_ref[...] * z_ref[...])).astype(o_ref.dtype)


def hy_inv_gate(y, plan, zb, u, gate_idx, bias_n, out_dtype, s=None, order=0):
    NB, P, C = zb.shape
    k2 = 2 * plan.K1P
    ghi, glo = _const_split(plan.g)
    tok = pl.BlockSpec((None, P, C), lambda b: (b, 0, 0))
    fused = s is not None
    in_specs = [pl.BlockSpec((None, k2, C), lambda b: (b, 0, 0)),
                pl.BlockSpec(ghi.shape, lambda b: (0, 0)), pl.BlockSpec(glo.shape, lambda b: (0, 0)),
                tok, pl.BlockSpec((None, None, P, C), lambda b: (gate_idx, b, 0, 0)),
                pl.BlockSpec((1, C), lambda b: (0, 0))]
    args = [y, ghi, glo, zb, u, bias_n.reshape(1, C)]
    scratch = []
    if fused:
        in_specs += [pl.BlockSpec((2, k2, C), lambda b: (order, 0, 0)), pl.BlockSpec((plan.K1P, LANES), lambda b: (0, 0))]
        args += [s, jnp.asarray(plan.sgn)]
        scratch = [pltpu.VMEM((k2, C), F32)]
    return pl.pallas_call(
        functools.partial(_inv_gate_kernel, k1p=plan.K1P, fused=fused),
        grid=(NB,),
        in_specs=in_specs,
        out_specs=tok,
        out_shape=jax.ShapeDtypeStruct((NB, P, C), out_dtype),
        scratch_shapes=scratch,
        compiler_params=_cparams("arbitrary" if fused else "parallel"),
        name="hy_inv_gate",
    )(*args)


def hyena_mixer(proj3, seq0, n_seq, col0, conv_w, conv_b, w1, b1, w2, b2, w3, freq, decay, bias):
    _, L, _ = proj3.shape
    width, out_dtype = HY_WIDTH, BF16
    P = min(L, HY_BLOCK)
    nb = L // P
    plan = HyPlan(P)
    u = hy_conv3(proj3, seq0, n_seq, col0, conv_w, conv_b, width).reshape(3, n_seq * nb, P, width)
    klin = hy_filters(L, w1, b1, w2, b2, w3, freq, decay, width)
    s = hy_fwd_dft(klin.reshape(HY_ORDER * 2 * nb, P, width), plan)
    z = u[0]
    for n in range(HY_ORDER):
        x = hy_fwd_dft(z, plan)
        od = out_dtype if n == HY_ORDER - 1 else F32
        if nb == 1:
            z = hy_inv_gate(x, plan, z, u, 1 + n, bias[n], od, s=s, order=n)
        else:
            z = hy_inv_gate(hy_block_mul(x, s, plan, nb, n), plan, z, u, 1 + n, bias[n], od)
    return z.reshape(n_seq, L, width)


def _segments(B):
    assert B % CHAINS == 0 or CHAINS % B == 0
    return 1 if B % CHAINS == 0 else CHAINS // B


def kernel(x_prompt, x_sample, cache_na_k, cache_na_v, state_s5, state_rglru, c, c_ctx, norm_g, mod_w, mod_b, ffn_wg, ffn_wu, ffn_wd, even_w_in, even_w_out, s5_lam_re, s5_lam_im, s5_log_dt, s5_b_re, s5_b_im, s5_c_re, s5_c_im, s5_d, s5_glu_w, s5_glu_b, hy_conv_w, hy_conv_b, hy_w1, hy_b1, hy_w2, hy_b2, hy_w3, hy_freq, hy_decay, hy_bias, odd_w_in, odd_w_out, rg_conv_w, rg_conv_b, rg_wa, rg_ba, rg_wi, rg_bi, rg_lam, na_qn, na_kn, na_rpb):
    bp, lp, _ = x_prompt.shape
    bs, ls, _ = x_sample.shape
    n_p = bp * lp
    n_s = bs * ls
    n_tok = n_p + n_s
    assert n_p % ls == 0
    groups = ((0, n_p, bp, lp), (n_p, n_s, bs, ls))

    cond = jnp.concatenate([c_ctx[None], c, jnp.zeros((COND_ROWS - 1 - bs, D_MODEL), F32)], axis=0)
    m_all = adaln_all(cond, mod_w, mod_b)

    wg = ffn_wg.astype(BF16)
    wu = ffn_wu.astype(BF16)
    wd = ffn_wd.astype(BF16)
    w_in_e = even_w_in.astype(BF16)
    w_out_e = even_w_out.astype(BF16)
    w_in_o = odd_w_in.astype(BF16)
    w_out_o = odd_w_out.astype(BF16)

    x = jnp.concatenate([x_prompt.reshape(n_p, D_MODEL), x_sample.reshape(n_s, D_MODEL)], axis=0)
    s5_new, rg_new, k_new, v_new = [], [], [], []
    for li in range(DEPTH):
        x, h = ffn_block(x, m_all, norm_g, wg, wu, wd, li, 0, n_p, ls)
        if li % 2 == 0:
            ei = li // 2
            proj = in_proj(h, w_in_e, ei)
            prm = s5_params(s5_lam_re[ei], s5_lam_im[ei], s5_log_dt[ei], s5_b_re[ei], s5_b_im[ei],
                            s5_c_re[ei], s5_c_im[ei])
            ya, yb = [], []
            for gi, (t0, nt, B, L) in enumerate(groups):
                S = _segments(B)
                nc, Ls = B * S, L // S
                u3 = proj.reshape(n_tok // Ls, Ls, -1)
                if gi == 0:
                    y, hfin = s5_scan(u3, t0 // Ls, nc, *prm, jnp.zeros((2, nc, 2 * S5_STATE), F32))
                    fin = hfin.reshape(2, B, 2, S5_GROUPS, S5_P)
                    s5_new.append(jnp.transpose(fin, (1, 0, 3, 4, 2)))
                else:
                    h0 = jnp.transpose(state_s5[:, ei], (1, 0, 4, 2, 3)).reshape(2, B, 2 * S5_STATE)
                    y = s5_segmented(u3, t0 // Ls, prm, h0, B, S)
                ya.append(s5_glu(proj, t0, y.reshape(2, nt, S5_WIDTH), s5_d[ei].reshape(-1), s5_glu_w[ei], s5_glu_b[ei]))
                yb.append(hyena_mixer(proj.reshape(n_tok // L, L, -1), t0 // L, B, 1, hy_conv_w[ei], hy_conv_b[ei],
                                      hy_w1[ei], hy_b1[ei], hy_w2[ei], hy_b2[ei], hy_w3[ei], hy_freq[ei],
                                      hy_decay[ei], hy_bias[ei]).reshape(nt, HY_WIDTH))
            x = out_proj_residual(ya, yb, w_out_e, ei, x, m_all, li, n_p, ls)
        else:
            oi = li // 2
            proj = in_proj(h, w_in_o, oi)
            gprm = rglru_params(rg_wa[oi], rg_ba[oi], rg_wi[oi], rg_bi[oi])
            q, k, v, k_f32 = qkv_prep(proj, 2, na_qn[oi], na_kn[oi])
            yc, yd = [], []
            for gi, (t0, nt, B, L) in enumerate(groups):
                if gi == 0:
                    h0 = jnp.zeros((2, B, RG_WIDTH), F32)
                else:
                    h0 = jnp.transpose(state_rglru[:, oi], (1, 0, 2))
                y_c, hfin = rglru_mixer(proj, t0, rg_conv_w[oi], rg_conv_b[oi], gprm, rg_lam[oi], h0, B, L,
                                        _segments(B))
                yc.append(y_c)
                if gi == 0:
                    rg_new.append(jnp.transpose(hfin, (1, 0, 2)))
                    yd.append(ctx_attention(q, k, v, B, L))
                    k_new.append(k_f32[:nt].reshape(B, L, NA_HEADS, NA_HD))
                    v_new.append(proj[:nt, 4 * NA_WIDTH:].reshape(B, L, NA_HEADS, NA_HD))
                else:
                    past = cache_na_k.shape[2]
                    kc = cache_na_k[:, oi].reshape(B, past, NA_WIDTH).astype(BF16)
                    vc = cache_na_v[:, oi].reshape(B, past, NA_WIDTH).astype(BF16)
                    yd.append(na_attention(q, k, v, kc, vc, na_rpb[oi], t0, B, L))
            x = out_proj_residual(yc, yd, w_out_o, oi, x, m_all, li, n_p, ls)
        (x,) = ffn_block(x, m_all, norm_g, wg, wu, wd, li, 1, n_p, ls)

    y_prompt = x[:n_p].reshape(bp, lp, D_MODEL)
    y_sample = x[n_p:].reshape(bs, ls, D_MODEL)
    new_na_k = jnp.stack(k_new, axis=1)
    new_na_v = jnp.stack(v_new, axis=1)
    new_s5 = jnp.stack(s5_new, axis=1)
    new_rglru = jnp.stack(rg_new, axis=1)
    return (y_prompt, y_sample, new_na_k, new_na_v, new_s5, new_rglru)
```
